```python
import math
import jax, jax.numpy as jnp
from jax import lax
import numpy as np

D_MODEL = 2048
BATCH = 2
SEQ = 16384
DEPTH = 2

CTX_LEN = 256
GRID_W = 64
Q_BLOCK = 128
ROPE_THETA = 10000.0
RMS_EPS = 1e-6

MIX_GROUP_W = D_MODEL // 4
MLA_HEADS = 4
MLA_V = MIX_GROUP_W // MLA_HEADS
MLA_NOPE = MLA_V
MLA_ROPE = MLA_NOPE // 2
MLA_Q_RANK = 3 * D_MODEL // 16
MLA_KV_RANK = D_MODEL // 16
MLA_IN = MLA_Q_RANK + MLA_KV_RANK + MLA_ROPE
GQA_HEADS = 4
GQA_KV_HEADS = 2
GQA_HD = MIX_GROUP_W // GQA_HEADS
GQA_IN = (GQA_HEADS + 2 * GQA_KV_HEADS) * GQA_HD
SSM_W = MIX_GROUP_W
SSM_GROUP_CH = 16
SSM_GROUPS = SSM_W // SSM_GROUP_CH
SSM_STATE = 64
SSM_DT_MIN = 0.001
SSM_DT_MAX = 0.1
FNO_W = MIX_GROUP_W
FNO_GROUPS = 4
FNO_GROUP_CH = FNO_W // FNO_GROUPS

OFF_B = MLA_IN
OFF_C = MLA_IN + GQA_IN
OFF_D = OFF_C + SSM_W
IN_COLS = OFF_D + FNO_W
D_MIX = MLA_HEADS * MLA_V + GQA_HEADS * GQA_HD + SSM_W + FNO_W

N_EXPERTS = 16
N_EXPERT_GROUPS = 4
EXPERTS_PER_GROUP = N_EXPERTS // N_EXPERT_GROUPS
TOP_K = 2
D_EXPERT = D_MODEL // 2
EXPERT_BLOCK = 128

kernel_name = "hybrid_mla_gqa_s5_fnet_moe_dit"


def rmsnorm(x, g):
    xf = x.astype(jnp.float32)
    y = xf * lax.rsqrt(jnp.mean(xf * xf, axis=-1, keepdims=True) + RMS_EPS)
    return (y * g.astype(jnp.float32)).astype(x.dtype)


def modulate(h, shift, scale):
    return h * (1.0 + scale) + shift


def axial_rope_tables(row, col, dim):
    quarter = dim // 4
    inv_freq = ROPE_THETA ** (-jnp.arange(quarter, dtype=jnp.float32) / quarter)
    ang = jnp.concatenate([row[:, None] * inv_freq, col[:, None] * inv_freq], axis=-1)
    return jnp.cos(ang), jnp.sin(ang)


def apply_rope(x, rope):
    cos, sin = rope
    cos = cos[None, :, None, :]
    sin = sin[None, :, None, :]
    x1, x2 = jnp.split(x.astype(jnp.float32), 2, axis=-1)
    return jnp.concatenate([x1 * cos - x2 * sin, x2 * cos + x1 * sin], axis=-1).astype(x.dtype)


def block_attention(q, k, v):
    b, s, hq, dq = q.shape
    hk, dv = k.shape[2], v.shape[-1]
    g = hq // hk
    nb = s // Q_BLOCK
    qb = q.reshape(b, nb, Q_BLOCK, hk, g, dq).transpose(1, 0, 2, 3, 4, 5)
    scale = dq ** -0.5

    def one_block(qblk):
        sc = jnp.einsum('bqhgd,bnhd->bhgqn', qblk, k).astype(jnp.float32) * scale
        p = jax.nn.softmax(sc, axis=-1).astype(v.dtype)
        return jnp.einsum('bhgqn,bnhe->bqhge', p, v)

    o = lax.map(one_block, qb)
    return o.transpose(1, 0, 2, 3, 4, 5).reshape(b, s, hq * dv)


def mla_q(p, g_q, w_uq, rope):
    b, n, _ = p.shape
    q = (rmsnorm(p[..., :MLA_Q_RANK], g_q) @ w_uq).reshape(b, n, MLA_HEADS, MLA_NOPE + MLA_ROPE)
    if rope is not None:
        q = jnp.concatenate([q[..., :MLA_NOPE], apply_rope(q[..., MLA_NOPE:], rope)], axis=-1)
    return q


def mla_kv(p, g_kv, w_ukv, rope):
    b, n, _ = p.shape
    kv_lat = p[..., MLA_Q_RANK:MLA_Q_RANK + MLA_KV_RANK]
    k_rot = p[..., MLA_Q_RANK + MLA_KV_RANK:MLA_IN][:, :, None, :]
    kv = (rmsnorm(kv_lat, g_kv) @ w_ukv).reshape(b, n, MLA_HEADS, MLA_NOPE + MLA_V)
    k_nope, v = kv[..., :MLA_NOPE], kv[..., MLA_NOPE:]
    if rope is not None:
        k_rot = apply_rope(k_rot, rope)
    k = jnp.concatenate([k_nope, jnp.broadcast_to(k_rot, (b, n, MLA_HEADS, MLA_ROPE))], axis=-1)
    return k, v


def gqa_q(p, g_q, rope):
    b, n, _ = p.shape
    q = rmsnorm(p[..., :GQA_HEADS * GQA_HD].reshape(b, n, GQA_HEADS, GQA_HD), g_q)
    return apply_rope(q, rope) if rope is not None else q


def gqa_kv(p, g_k, rope):
    b, n, _ = p.shape
    k_end = (GQA_HEADS + GQA_KV_HEADS) * GQA_HD
    k = rmsnorm(p[..., GQA_HEADS * GQA_HD:k_end].reshape(b, n, GQA_KV_HEADS, GQA_HD), g_k)
    v = p[..., k_end:].reshape(b, n, GQA_KV_HEADS, GQA_HD)
    if rope is not None:
        k = apply_rope(k, rope)
    return k, v


def s5_discretise(lam_re, lam_im, log_dt, b_re, b_im):
    lam = lax.complex(lam_re.astype(jnp.float32), lam_im.astype(jnp.float32))
    dt = jnp.exp(log_dt.astype(jnp.float32))[:, None]
    lam_bar = jnp.exp(lam * dt)
    bmat = lax.complex(b_re.astype(jnp.float32), b_im.astype(jnp.float32))
    b_bar = ((lam_bar - 1.0) / lam)[..., None] * bmat
    return lam_bar, b_bar


def diag_scan(lam_bar, bu, reverse):
    a = jnp.broadcast_to(lam_bar, bu.shape)

    def combine(e1, e2):
        a1, b1 = e1
        a2, b2 = e2
        return a1 * a2, a2 * b1 + b2

    return lax.associative_scan(combine, (a, bu), reverse=reverse, axis=1)[1]


def s5_glu(y, w_glu, dtype):
    z = jax.nn.gelu(y).astype(dtype) @ w_glu
    a, b = jnp.split(z, 2, axis=-1)
    return a * jax.nn.sigmoid(b)


def s5_mixer(u_l, u_c, lam_re, lam_im, log_dt, b_re, b_im, c_re, c_im, d, w_glu, need_ctx):
    b, n, _ = u_l.shape
    ul = u_l.astype(jnp.float32).reshape(b, n, SSM_GROUPS, SSM_GROUP_CH).astype(jnp.complex64)
    uc = u_c.astype(jnp.float32).reshape(b, u_c.shape[1], SSM_GROUPS, SSM_GROUP_CH).astype(jnp.complex64)
    dg = d.astype(jnp.float32).reshape(SSM_GROUPS, SSM_GROUP_CH)
    y_l = dg * ul.real
    y_c = dg * uc.real if need_ctx else None
    for direction in range(2):
        reverse = direction == 1
        lam_bar, b_bar = s5_discretise(lam_re[direction], lam_im[direction], log_dt[direction],
                                       b_re[direction], b_im[direction])
        c_mat = lax.complex(c_re[direction].astype(jnp.float32), c_im[direction].astype(jnp.float32))
        h_c = diag_scan(lam_bar, jnp.einsum('bngh,gph->bngp', uc, b_bar), reverse)
        h_end = h_c[:, 0] if reverse else h_c[:, -1]
        bu_l = jnp.einsum('bngh,gph->bngp', ul, b_bar)
        bu_l = bu_l.at[:, -1 if reverse else 0].add(lam_bar * h_end)
        h_l = diag_scan(lam_bar, bu_l, reverse)
        y_l = y_l + jnp.einsum('bngp,ghp->bngh', h_l, c_mat).real
        if need_ctx:
            y_c = y_c + jnp.einsum('bngp,ghp->bngh', h_c, c_mat).real
    out_l = s5_glu(y_l.reshape(b, n, SSM_W), w_glu, u_l.dtype)
    out_c = s5_glu(y_c.reshape(b, -1, SSM_W), w_glu, u_c.dtype) if need_ctx else None
    return out_l, out_c


def fourier_mix(p):
    b, n, _ = p.shape
    f = p.astype(jnp.float32).reshape(b, n, FNO_GROUPS, FNO_GROUP_CH)
    return jnp.fft.fft2(f, axes=(1, 3), norm='ortho').real.reshape(b, n, FNO_W).astype(p.dtype)


def token_mixers(hl, hc, w_in, mla_q_norm_g, mla_w_uq, mla_kv_norm_g, mla_w_ukv,
                 gqa_q_norm_g, gqa_k_norm_g, ssm_lam_re, ssm_lam_im, ssm_log_dt,
                 ssm_b_re, ssm_b_im, ssm_c_re, ssm_c_im, ssm_d, ssm_w_glu,
                 rope_mla, rope_gqa, need_ctx):
    pl = hl @ w_in
    pc = hc @ (w_in if need_ctx else w_in[:, :OFF_D])
    pl_a, pl_b, pl_c, pl_d = pl[..., :OFF_B], pl[..., OFF_B:OFF_C], pl[..., OFF_C:OFF_D], pl[..., OFF_D:]
    pc_a, pc_b, pc_c = pc[..., :OFF_B], pc[..., OFF_B:OFF_C], pc[..., OFF_C:OFF_D]

    ka_c, va_c = mla_kv(pc_a, mla_kv_norm_g, mla_w_ukv, None)
    ka_l, va_l = mla_kv(pl_a, mla_kv_norm_g, mla_w_ukv, rope_mla)
    out_a = block_attention(mla_q(pl_a, mla_q_norm_g, mla_w_uq, rope_mla),
                            jnp.concatenate([ka_c, ka_l], axis=1), jnp.concatenate([va_c, va_l], axis=1))

    kb_c, vb_c = gqa_kv(pc_b, gqa_k_norm_g, None)
    kb_l, vb_l = gqa_kv(pl_b, gqa_k_norm_g, rope_gqa)
    out_b = block_attention(gqa_q(pl_b, gqa_q_norm_g, rope_gqa),
                            jnp.concatenate([kb_c, kb_l], axis=1), jnp.concatenate([vb_c, vb_l], axis=1))

    out_c_l, out_c_c = s5_mixer(pl_c, pc_c, ssm_lam_re, ssm_lam_im, ssm_log_dt, ssm_b_re, ssm_b_im,
                                ssm_c_re, ssm_c_im, ssm_d, ssm_w_glu, need_ctx)

    out_d = fourier_mix(pl_d)

    mix_l = jnp.concatenate([out_a, out_b, out_c_l, out_d], axis=-1)
    mix_c = None
    if need_ctx:
        ctx_a = block_attention(mla_q(pc_a, mla_q_norm_g, mla_w_uq, None), ka_c, va_c)
        ctx_b = block_attention(gqa_q(pc_b, gqa_q_norm_g, None), kb_c, vb_c)
        mix_c = jnp.concatenate([ctx_a, ctx_b, out_c_c, fourier_mix(pc[..., OFF_D:])], axis=-1)
    return mix_l, mix_c


def routed_moe(h, router_w, router_bias, w_gate, w_up, w_down):
    t, d = h.shape
    scores = jax.nn.sigmoid((h @ router_w).astype(jnp.float32))
    sel = (scores + router_bias.astype(jnp.float32)).reshape(t, N_EXPERT_GROUPS, EXPERTS_PER_GROUP)
    grp = jnp.argmax(lax.top_k(sel, TOP_K)[0].sum(-1), axis=-1)
    sel_in = jnp.take_along_axis(sel, grp[:, None, None], axis=1)[:, 0]
    _, local = lax.top_k(sel_in, TOP_K)
    expert = grp[:, None] * EXPERTS_PER_GROUP + local
    gate = jnp.take_along_axis(scores, expert, axis=1)
    gate = gate / jnp.sum(gate, axis=-1, keepdims=True)

    flat_e = expert.reshape(-1)
    flat_tok = jnp.repeat(jnp.arange(t, dtype=jnp.int32), TOP_K)
    flat_g = gate.reshape(-1)
    order = jnp.argsort(flat_e)
    e_sorted = flat_e[order]
    counts = jnp.bincount(flat_e, length=N_EXPERTS)
    padded = (counts + EXPERT_BLOCK - 1) // EXPERT_BLOCK * EXPERT_BLOCK
    pad_end = jnp.cumsum(padded)
    pad_start = pad_end - padded
    start = jnp.cumsum(counts) - counts
    dest = pad_start[e_sorted] + jnp.arange(t * TOP_K, dtype=jnp.int32) - start[e_sorted]
    n_blocks = (t * TOP_K + N_EXPERTS * (EXPERT_BLOCK - 1)) // EXPERT_BLOCK + 1
    n_rows = n_blocks * EXPERT_BLOCK
    row_tok = jnp.zeros((n_rows,), jnp.int32).at[dest].set(flat_tok[order])
    row_gate = jnp.zeros((n_rows,), jnp.float32).at[dest].set(flat_g[order])
    block_expert = jnp.minimum(
        jnp.searchsorted(pad_end, jnp.arange(n_blocks, dtype=jnp.int32) * EXPERT_BLOCK, side='right'),
        N_EXPERTS - 1)
    xb = h[row_tok].reshape(n_blocks, EXPERT_BLOCK, d)

    def expert_block(args):
        xblk, e = args
        return (jax.nn.silu(xblk @ w_gate[e]) * (xblk @ w_up[e])) @ w_down[e]

    yb = lax.map(expert_block, (xb, block_expert)).reshape(n_rows, d)
    return jnp.zeros_like(h).at[row_tok].add(yb * row_gate[:, None].astype(h.dtype))


def setup_inputs(seed: int = 0) -> dict:
    key = jax.random.key(seed)
    ks = jax.random.split(key, 31)
    f32 = jnp.float32

    def nrm(k, shape, std):
        return jax.random.normal(k, shape, f32) * std

    L, D = DEPTH, D_MODEL
    G, P, H = SSM_GROUPS, SSM_STATE, SSM_GROUP_CH
    return {
        'x': nrm(ks[0], (BATCH, SEQ, D), 1.0),
        'c': nrm(ks[1], (BATCH, D), 1.0),
        'ctx': nrm(ks[2], (BATCH, CTX_LEN, D), 1.0),
        'c_ctx': nrm(ks[3], (D,), 1.0),
        'ada_w': nrm(ks[4], (L, D, 6 * D), 0.5 * D ** -0.5),
        'ada_b': nrm(ks[5], (L, 6 * D), 0.02),
        'norm1_g': 1.0 + nrm(ks[6], (L, D), 0.02),
        'norm2_g': 1.0 + nrm(ks[7], (L, D), 0.02),
        'w_in': nrm(ks[8], (L, D, IN_COLS), D ** -0.5),
        'mla_q_norm_g': 1.0 + nrm(ks[9], (L, MLA_Q_RANK), 0.02),
        'mla_w_uq': nrm(ks[10], (L, MLA_Q_RANK, MLA_HEADS * (MLA_NOPE + MLA_ROPE)), MLA_Q_RANK ** -0.5),
        'mla_kv_norm_g': 1.0 + nrm(ks[11], (L, MLA_KV_RANK), 0.02),
        'mla_w_ukv': nrm(ks[12], (L, MLA_KV_RANK, MLA_HEADS * (MLA_NOPE + MLA_V)), MLA_KV_RANK ** -0.5),
        'gqa_q_norm_g': 1.0 + nrm(ks[13], (L, GQA_HD), 0.02),
        'gqa_k_norm_g': 1.0 + nrm(ks[14], (L, GQA_HD), 0.02),
        'ssm_lam_re': -0.5 + nrm(ks[15], (L, 2, G, P), 0.01),
        'ssm_lam_im': jnp.pi * jnp.arange(P, dtype=f32) + nrm(ks[16], (L, 2, G, P), 0.01),
        'ssm_log_dt': jax.random.uniform(ks[17], (L, 2, G), f32, math.log(SSM_DT_MIN), math.log(SSM_DT_MAX)),
        'ssm_b_re': nrm(ks[18], (L, 2, G, P, H), (2 * H) ** -0.5),
        'ssm_b_im': nrm(ks[19], (L, 2, G, P, H), (2 * H) ** -0.5),
        'ssm_c_re': nrm(ks[20], (L, 2, G, H, P), P ** -0.5),
        'ssm_c_im': nrm(ks[21], (L, 2, G, H, P), P ** -0.5),
        'ssm_d': nrm(ks[22], (L, SSM_W), 0.5),
        'ssm_w_glu': nrm(ks[23], (L, SSM_W, 2 * SSM_W), SSM_W ** -0.5),
        'w_out': nrm(ks[24], (L, D_MIX, D), D_MIX ** -0.5),
        'router_w': nrm(ks[25], (D, N_EXPERTS), D ** -0.5),
        'router_bias': nrm(ks[26], (N_EXPERTS,), 0.01),
        'moe_w_gate': nrm(ks[27], (L, N_EXPERTS, D, D_EXPERT), D ** -0.5),
        'moe_w_up': nrm(ks[28], (L, N_EXPERTS, D, D_EXPERT), D ** -0.5),
        'moe_w_down': nrm(ks[29], (L, N_EXPERTS, D_EXPERT, D), D_EXPERT ** -0.5),
        'final_norm_g': 1.0 + nrm(ks[30], (D,), 0.02),
    }


def reference(x, c, ctx, c_ctx, ada_w, ada_b, norm1_g, norm2_g, w_in, mla_q_norm_g, mla_w_uq,
              mla_kv_norm_g, mla_w_ukv, gqa_q_norm_g, gqa_k_norm_g, ssm_lam_re, ssm_lam_im,
              ssm_log_dt, ssm_b_re, ssm_b_im, ssm_c_re, ssm_c_im, ssm_d, ssm_w_glu, w_out,
              router_w, router_bias, moe_w_gate, moe_w_up, moe_w_down, final_norm_g):
    b, s, d = x.shape
    rows = s // GRID_W
    row = jnp.repeat(jnp.arange(rows), GRID_W).astype(jnp.float32)
    col = jnp.tile(jnp.arange(GRID_W), rows).astype(jnp.float32)
    rope_mla = axial_rope_tables(row, col, MLA_ROPE)
    rope_gqa = axial_rope_tables(row, col, GQA_HD)
    silu_c = jax.nn.silu(c)
    silu_cc = jax.nn.silu(c_ctx)

    for l in range(DEPTH):
        need_ctx = l < DEPTH - 1
        sh1, sc1, g1, sh2, sc2, g2 = [m[:, None, :] for m in jnp.split(silu_c @ ada_w[l] + ada_b[l], 6, axis=-1)]
        csh1, csc1, cg1, csh2, csc2, cg2 = jnp.split(silu_cc @ ada_w[l] + ada_b[l], 6, axis=-1)

        hl = modulate(rmsnorm(x, norm1_g[l]), sh1, sc1)
        hc = modulate(rmsnorm(ctx, norm1_g[l]), csh1, csc1)
        mix_l, mix_c = token_mixers(hl, hc, w_in[l], mla_q_norm_g[l], mla_w_uq[l], mla_kv_norm_g[l],
                                    mla_w_ukv[l], gqa_q_norm_g[l], gqa_k_norm_g[l], ssm_lam_re[l],
                                    ssm_lam_im[l], ssm_log_dt[l], ssm_b_re[l], ssm_b_im[l], ssm_c_re[l],
                                    ssm_c_im[l], ssm_d[l], ssm_w_glu[l], rope_mla, rope_gqa, need_ctx)
        x = x + g1 * (mix_l @ w_out[l])
        hl2 = modulate(rmsnorm(x, norm2_g[l]), sh2, sc2)
        if need_ctx:
            ctx = ctx + cg1 * (mix_c @ w_out[l])
            hc2 = modulate(rmsnorm(ctx, norm2_g[l]), csh2, csc2)
            n_ctx = hc2.shape[0] * hc2.shape[1]
            tokens = jnp.concatenate([hc2.reshape(-1, d), hl2.reshape(-1, d)], axis=0)
            y = routed_moe(tokens, router_w, router_bias, moe_w_gate[l], moe_w_up[l], moe_w_down[l])
            ctx = ctx + cg2 * y[:n_ctx].reshape(ctx.shape)
            x = x + g2 * y[n_ctx:].reshape(x.shape)
        else:
            y = routed_moe(hl2.reshape(-1, d), router_w, router_bias, moe_w_gate[l], moe_w_up[l], moe_w_down[l])
            x = x + g2 * y.reshape(x.shape)

    return rmsnorm(x, final_norm_g)
```

```python
import functools
import math

import jax
import jax.numpy as jnp
from jax import lax
from jax.experimental import pallas as pl
from jax.experimental.pallas import tpu as pltpu

F32 = jnp.float32
BF16 = jnp.bfloat16

D_MODEL = 2048
GRID_W = 64
ROPE_THETA = 10000.0
RMS_EPS = 1e-6
MIX_W = D_MODEL // 4
MLA_HEADS = 4
MLA_NOPE = 128
MLA_V = 128
MLA_ROPE = 64
MLA_Q_RANK = 384
MLA_KV_RANK = 128
MLA_IN = MLA_Q_RANK + MLA_KV_RANK + MLA_ROPE
MLA_DQ = 256
GQA_HEADS = 4
GQA_KV_HEADS = 2
GQA_HD = 128
GQA_IN = (GQA_HEADS + 2 * GQA_KV_HEADS) * GQA_HD
SSM_GROUPS = 32
SSM_CH = 16
SSM_STATE = 64
SSM_CHUNK = 16
FNO_GROUPS = 4
FNO_CH = 128
OFF_B = MLA_IN
OFF_C = OFF_B + GQA_IN
OFF_D = OFF_C + MIX_W
N_EXPERTS = 16
N_EXPERT_GROUPS = 4
EXPERTS_PER_GROUP = 4
D_EXPERT = D_MODEL // 2
MOE_BLOCK = 256
LANE = 128
VMEM_LIMIT = 52 * 1024 * 1024


def _cp(sem, vmem=VMEM_LIMIT):
    return pltpu.CompilerParams(dimension_semantics=sem, vmem_limit_bytes=vmem)


def _const_spec(shape):
    nd = len(shape)
    return pl.BlockSpec(shape, lambda *_: (0,) * nd)


def _split_bf16(a):
    hi = a.astype(BF16)
    lo = (a - hi.astype(F32)).astype(BF16)
    return hi, lo


def _dot(a, b):
    return jnp.dot(a, b, preferred_element_type=F32)


def _dot3(a, b):
    ah, al = _split_bf16(a)
    bh, bl = _split_bf16(b)
    return _dot(ah, bh) + _dot(al, bh) + _dot(ah, bl)


def _rms(x, g):
    return x * lax.rsqrt(jnp.mean(x * x, axis=-1, keepdims=True) + RMS_EPS) * g


def _mod_kernel(s_ref, w_ref, b_ref, o_ref):
    o_ref[...] = _dot3(s_ref[...], w_ref[...]) + b_ref[...]


def mod_vectors(s8, w, b):
    k, n = w.shape
    tn = 1024
    return pl.pallas_call(
        _mod_kernel,
        out_shape=jax.ShapeDtypeStruct((8, n), F32),
        grid=(n // tn,),
        in_specs=[_const_spec((8, k)), pl.BlockSpec((k, tn), lambda j: (0, j)), pl.BlockSpec((1, tn), lambda j: (0, j))],
        out_specs=pl.BlockSpec((8, tn), lambda j: (0, j)),
        compiler_params=_cp(("parallel",)),
        name="mod_vectors",
    )(s8, w, b.reshape(1, n))


def _inproj_kernel(x_ref, g_ref, sh_ref, sc_ref, w_ref, *o_refs, widths):
    h = _rms(x_ref[...], g_ref[...])
    hb = (h * (1.0 + sc_ref[0]) + sh_ref[0]).astype(BF16)
    off = 0
    for o_ref, wd in zip(o_refs, widths):
        o_ref[...] = _dot(hb, w_ref[:, off:off + wd]).astype(o_ref.dtype)
        off += wd


def in_projection(tok, gamma, shift, scale, w, widths, lat_tiles_per_batch, n_batch, tm):
    t, d = tok.shape
    mod_idx = lambda i: (jnp.minimum(i // lat_tiles_per_batch, n_batch), 0, 0)
    return pl.pallas_call(
        functools.partial(_inproj_kernel, widths=widths),
        out_shape=[jax.ShapeDtypeStruct((t, wd), BF16) for wd in widths],
        grid=(t // tm,),
        in_specs=[pl.BlockSpec((tm, d), lambda i: (i, 0)), _const_spec((1, d)),
                  pl.BlockSpec((1, 1, d), mod_idx), pl.BlockSpec((1, 1, d), mod_idx),
                  pl.BlockSpec(w.shape, lambda i: (0, 0), pipeline_mode=pl.Buffered(1))],
        out_specs=[pl.BlockSpec((tm, wd), lambda i: (i, 0)) for wd in widths],
        compiler_params=_cp(("parallel",)),
        name="in_projection",
    )(tok, gamma.reshape(1, d), shift, scale, w)


def _qkv_kernel(a_ref, b_ref, cm_ref, sm_ref, cg_ref, sg_ref, gq_ref, gkv_ref, ggq_ref, ggk_ref, wuq_ref, wukv_ref,
                qm_ref, km_ref, vm_ref, qg_ref, kg_ref, vg_ref):
    a = a_ref[...].astype(F32)
    cm, sm = cm_ref[...], sm_ref[...]
    mla_scale = (MLA_NOPE + MLA_ROPE) ** -0.5
    qn = _rms(a[:, :MLA_Q_RANK], gq_ref[...]).astype(BF16)
    q_all = _dot(qn, wuq_ref[...])
    rot0 = MLA_HEADS * MLA_DQ
    for h in range(MLA_HEADS):
        nope = q_all[:, MLA_DQ * h:MLA_DQ * h + LANE]
        rp = q_all[:, MLA_DQ * h + LANE:MLA_DQ * (h + 1)]
        rr = q_all[:, rot0 + LANE * h:rot0 + LANE * (h + 1)]
        qm_ref[0, h, :, :LANE] = (nope * mla_scale).astype(BF16)
        qm_ref[0, h, :, LANE:] = ((rp * cm + rr * sm) * mla_scale).astype(BF16)
    kvn = _rms(a[:, MLA_Q_RANK:MLA_Q_RANK + MLA_KV_RANK], gkv_ref[...]).astype(BF16)
    kv = _dot(kvn, wukv_ref[...])
    kr = (a[:, 512:640] * cm + a[:, 640:768] * sm).astype(BF16)
    for h in range(MLA_HEADS):
        km_ref[0, h, :, :LANE] = kv[:, 256 * h:256 * h + LANE].astype(BF16)
        km_ref[0, h, :, LANE:] = kr
        vm_ref[0, h] = kv[:, 256 * h + LANE:256 * (h + 1)].astype(BF16)
    b = b_ref[...].astype(F32)
    cg, sg = cg_ref[...], sg_ref[...]
    gqa_scale = GQA_HD ** -0.5
    grp = GQA_HEADS // GQA_KV_HEADS
    for h in range(GQA_HEADS):
        qh = _rms(b[:, GQA_HD * h:GQA_HD * (h + 1)], ggq_ref[...])
        qh = qh * cg + pltpu.roll(qh, GQA_HD // 2, 1) * sg
        qg_ref[0, h // grp, h % grp] = (qh * gqa_scale).astype(BF16)
    k0 = GQA_HEADS * GQA_HD
    v0 = k0 + GQA_KV_HEADS * GQA_HD
    for h in range(GQA_KV_HEADS):
        kh = _rms(b[:, k0 + GQA_HD * h:k0 + GQA_HD * (h + 1)], ggk_ref[...])
        kg_ref[0, h] = (kh * cg + pltpu.roll(kh, GQA_HD // 2, 1) * sg).astype(BF16)
        vg_ref[0, h] = b_ref[:, v0 + GQA_HD * h:v0 + GQA_HD * (h + 1)]


def qkv_prepare(pa, pb, tabs, g_q, g_kv, g_gq, g_gk, w_uq, w_ukv, n_batch, seq, ctx_len, tm):
    t = pa.shape[0]
    n = seq + ctx_len
    lat_tiles = seq // tm
    ctx_tiles = ctx_len // tm
    n_lat = n_batch * lat_tiles

    def bidx(i):
        j = i - n_lat
        bi = jnp.where(i < n_lat, i // lat_tiles, j // ctx_tiles)
        blk = jnp.where(i < n_lat, i % lat_tiles, lat_tiles + j % ctx_tiles)
        return bi, blk

    def o4(i):
        bi, blk = bidx(i)
        return (bi, 0, blk, 0)

    def o5(i):
        bi, blk = bidx(i)
        return (bi, 0, 0, blk, 0)

    tab_spec = pl.BlockSpec((tm, LANE), lambda i: (bidx(i)[1], 0))
    vec = lambda w: _const_spec((1, w))
    return pl.pallas_call(
        _qkv_kernel,
        out_shape=[jax.ShapeDtypeStruct((n_batch, MLA_HEADS, n, MLA_DQ), BF16),
                   jax.ShapeDtypeStruct((n_batch, MLA_HEADS, n, MLA_DQ), BF16),
                   jax.ShapeDtypeStruct((n_batch, MLA_HEADS, n, MLA_V), BF16),
                   jax.ShapeDtypeStruct((n_batch, GQA_KV_HEADS, GQA_HEADS // GQA_KV_HEADS, n, GQA_HD), BF16),
                   jax.ShapeDtypeStruct((n_batch, GQA_KV_HEADS, n, GQA_HD), BF16),
                   jax.ShapeDtypeStruct((n_batch, GQA_KV_HEADS, n, GQA_HD), BF16)],
        grid=(t // tm,),
        in_specs=[pl.BlockSpec((tm, pa.shape[1]), lambda i: (i, 0)), pl.BlockSpec((tm, pb.shape[1]), lambda i: (i, 0)),
                  tab_spec, tab_spec, tab_spec, tab_spec,
                  vec(MLA_Q_RANK), vec(MLA_KV_RANK), vec(GQA_HD), vec(GQA_HD),
                  _const_spec(w_uq.shape), _const_spec(w_ukv.shape)],
        out_specs=[pl.BlockSpec((1, MLA_HEADS, tm, MLA_DQ), o4), pl.BlockSpec((1, MLA_HEADS, tm, MLA_DQ), o4),
                   pl.BlockSpec((1, MLA_HEADS, tm, MLA_V), o4),
                   pl.BlockSpec((1, GQA_KV_HEADS, GQA_HEADS // GQA_KV_HEADS, tm, GQA_HD), o5),
                   pl.BlockSpec((1, GQA_KV_HEADS, tm, GQA_HD), o4), pl.BlockSpec((1, GQA_KV_HEADS, tm, GQA_HD), o4)],
        compiler_params=_cp(("parallel",)),
        name="qkv_prepare",
    )(pa, pb, *tabs, g_q.reshape(1, -1), g_kv.reshape(1, -1), g_gq.reshape(1, -1), g_gk.reshape(1, -1), w_uq, w_ukv)


def _flash_kernel(q_ref, k_ref, v_ref, o_ref, m_sc, l_sc, acc_sc, *, grp, tq, dv, n_kv):
    j = pl.program_id(3)

    @pl.when(j == 0)
    def _():
        m_sc[...] = jnp.full_like(m_sc, -jnp.inf)
        l_sc[...] = jnp.zeros_like(l_sc)
        acc_sc[...] = jnp.zeros_like(acc_sc)

    q = q_ref[0, 0].reshape(grp * tq, q_ref.shape[-1])
    s = lax.dot_general(q, k_ref[0, 0], (((1,), (1,)), ((), ())), preferred_element_type=F32)
    m_prev = m_sc[...]
    m_new = jnp.maximum(m_prev, jnp.max(s, axis=-1, keepdims=True))
    alpha = jnp.exp(m_prev - m_new)
    p = jnp.exp(s - m_new)
    l_sc[...] = alpha * l_sc[...] + jnp.sum(p, axis=-1, keepdims=True)
    acc_sc[...] = alpha * acc_sc[...] + _dot(p.astype(BF16), v_ref[0, 0])
    m_sc[...] = m_new

    @pl.when(j == n_kv - 1)
    def _():
        o = acc_sc[...] / l_sc[...]
        for g in range(grp):
            o_ref[0, :, dv * g:dv * (g + 1)] = o[tq * g:tq * (g + 1)].astype(o_ref.dtype)


def flash_attention(q, k, v, q_row0, n_q, n_keys, tq, tk):
    nb, hk, grp, n, dq = q.shape
    dv = v.shape[-1]
    k_row0 = n - n_keys if n_keys < n else 0
    qb0, kb0, n_kv = q_row0 // tq, k_row0 // tk, n_keys // tk
    return pl.pallas_call(
        functools.partial(_flash_kernel, grp=grp, tq=tq, dv=dv, n_kv=n_kv),
        out_shape=jax.ShapeDtypeStruct((nb, n_q, hk * grp * dv), BF16),
        grid=(nb, hk, n_q // tq, n_kv),
        in_specs=[pl.BlockSpec((1, 1, grp, tq, dq), lambda b, h, i, j: (b, h, 0, qb0 + i, 0)),
                  pl.BlockSpec((1, 1, tk, dq), lambda b, h, i, j: (b, h, kb0 + j, 0)),
                  pl.BlockSpec((1, 1, tk, dv), lambda b, h, i, j: (b, h, kb0 + j, 0))],
        out_specs=pl.BlockSpec((1, tq, grp * dv), lambda b, h, i, j: (b, i, h)),
        scratch_shapes=[pltpu.VMEM((grp * tq, 1), F32), pltpu.VMEM((grp * tq, 1), F32), pltpu.VMEM((grp * tq, dv), F32)],
        compiler_params=_cp(("parallel", "parallel", "parallel", "arbitrary")),
        name="flash_attention",
    )(q, k, v)


def s5_tables(lam_re, lam_im, log_dt, b_re, b_im, c_re, c_im, d):
    hp = lax.Precision.HIGHEST
    tc = SSM_CHUNK
    lam = lax.complex(lam_re.astype(F32), lam_im.astype(F32))
    dt = jnp.exp(log_dt.astype(F32))[..., None]
    lam_bar = jnp.exp(lam * dt)
    b_bar = ((lam_bar - 1.0) / lam)[..., None] * lax.complex(b_re.astype(F32), b_im.astype(F32))
    c_mat = lax.complex(c_re.astype(F32), c_im.astype(F32))
    steps = jnp.arange(tc + 1, dtype=F32)
    lam_pow = jnp.exp((lam * dt)[:, :, None, :] * steps[None, None, :, None])
    kern = jnp.einsum('dgip,dgtp,dgpj->dgtij', c_mat, lam_pow[:, :, :tc], b_bar, precision=hp).real
    s_idx = jnp.arange(tc)[:, None]
    t_idx = jnp.arange(tc)[None, :]
    kf = kern[0][:, jnp.clip(t_idx - s_idx, 0, tc - 1)] * (t_idx >= s_idx)[None, :, :, None, None]
    kb = kern[1][:, jnp.clip(s_idx - t_idx, 0, tc - 1)] * (s_idx >= t_idx)[None, :, :, None, None]
    dg = d.astype(F32).reshape(SSM_GROUPS, SSM_CH)
    skip = (s_idx == t_idx)[None, :, :, None, None] * (jnp.eye(SSM_CH, dtype=F32) * dg[:, :, None])[:, None, None]
    w_t = (kf + kb + skip).transpose(0, 1, 4, 2, 3).reshape(SSM_GROUPS, tc * SSM_CH, tc * SSM_CH)
    zf = lam_pow[0][:, tc - 1 - jnp.arange(tc)][..., None] * b_bar[0][:, None]
    zb = lam_pow[1][:, jnp.arange(tc)][..., None] * b_bar[1][:, None]
    to_rows = lambda z: z.transpose(0, 1, 3, 2).reshape(SSM_GROUPS, tc * SSM_CH, SSM_STATE)
    w_z = jnp.concatenate([to_rows(zf.real), to_rows(zb.real), to_rows(zf.imag), to_rows(zb.imag)], axis=-1)
    mf = c_mat[0][:, None] * lam_pow[0][:, 1 + jnp.arange(tc)][:, :, None, :]
    mb = c_mat[1][:, None] * lam_pow[1][:, tc - jnp.arange(tc)][:, :, None, :]
    to_cols = lambda m: m.transpose(0, 3, 1, 2).reshape(SSM_GROUPS, SSM_STATE, tc * SSM_CH)
    w_c = jnp.concatenate([to_cols(mf.real), to_cols(mb.real), -to_cols(mf.imag), -to_cols(mb.imag)], axis=1)
    a16 = lam_pow[:, :, tc]
    a_re = jnp.concatenate([a16[0].real, a16[1].real], axis=-1)
    a_im = jnp.concatenate([a16[0].imag, a16[1].imag], axis=-1)
    return w_t.astype(BF16), w_z.astype(BF16), w_c.astype(BF16), a_re, a_im


def _s5_z_kernel(u_ref, w_ref, z_ref):
    z_ref[0] = _dot(u_ref[0], w_ref[0])


def s5_chunk_sums(u, w_z):
    bg, kc, cw = u.shape
    return pl.pallas_call(
        _s5_z_kernel,
        out_shape=jax.ShapeDtypeStruct((bg, kc, cw), F32),
        grid=(bg,),
        in_specs=[pl.BlockSpec((1, kc, cw), lambda i: (i, 0, 0)), pl.BlockSpec((1, cw, cw), lambda i: (i % SSM_GROUPS, 0, 0))],
        out_specs=pl.BlockSpec((1, kc, cw), lambda i: (i, 0, 0)),
        compiler_params=_cp(("parallel",)),
        name="s5_chunk_sums",
    )(u, w_z)


def _s5_scan_kernel(z_ref, ar_ref, ai_ref, h_ref, *, k_lat, k_ctx):
    ar, ai = ar_ref[...], ai_ref[...]
    rows = ar.shape[0]
    fwd_lanes = lax.broadcasted_iota(jnp.int32, (rows, LANE), 1) < SSM_STATE

    def step(cf, cb, hr, hi):
        h_ref[cf, :, 0:SSM_STATE] = hr[:, :SSM_STATE]
        h_ref[cb, :, SSM_STATE:LANE] = hr[:, SSM_STATE:]
        h_ref[cf, :, LANE:LANE + SSM_STATE] = hi[:, :SSM_STATE]
        h_ref[cb, :, LANE + SSM_STATE:] = hi[:, SSM_STATE:]
        zf, zb = z_ref[cf], z_ref[cb]
        zr = jnp.where(fwd_lanes, zf[:, :LANE], zb[:, :LANE])
        zi = jnp.where(fwd_lanes, zf[:, LANE:], zb[:, LANE:])
        return ar * hr - ai * hi + zr, ar * hi + ai * hr + zi

    def ctx_body(n, c):
        return step(k_lat + n, k_lat + k_ctx - 1 - n, *c)

    def lat_body(n, c):
        return step(n, k_lat - 1 - n, *c)

    zero = jnp.zeros((rows, LANE), F32)
    carry = lax.fori_loop(0, k_ctx, ctx_body, (zero, zero))
    lax.fori_loop(0, k_lat, lat_body, carry)


def s5_state_scan(z, a_re, a_im, k_lat, k_ctx):
    kc, r, cw = z.shape
    tr = 8
    return pl.pallas_call(
        functools.partial(_s5_scan_kernel, k_lat=k_lat, k_ctx=k_ctx),
        out_shape=jax.ShapeDtypeStruct((kc, r, cw), F32),
        grid=(r // tr,),
        in_specs=[pl.BlockSpec((kc, tr, cw), lambda i: (0, i, 0)), pl.BlockSpec((tr, LANE), lambda i: (i, 0)),
                  pl.BlockSpec((tr, LANE), lambda i: (i, 0))],
        out_specs=pl.BlockSpec((kc, tr, cw), lambda i: (0, i, 0)),
        compiler_params=_cp(("parallel",)),
        name="s5_state_scan",
    )(z, a_re, a_im)


def _gelu_tanh(y):
    return 0.5 * y * (1.0 + jnp.tanh(math.sqrt(2.0 / math.pi) * (y + 0.044715 * (y * y * y))))


def _s5_y_kernel(u_ref, h_ref, wt_ref, wc_ref, y_ref):
    y = _dot(u_ref[0], wt_ref[0]) + _dot(h_ref[0].astype(BF16), wc_ref[0])
    y_ref[0] = _gelu_tanh(y).astype(y_ref.dtype)


def s5_outputs(u, h, w_t, w_c):
    bg, kc, cw = u.shape
    blk = lambda: pl.BlockSpec((1, kc, cw), lambda i: (i, 0, 0))
    wsp = lambda: pl.BlockSpec((1, cw, cw), lambda i: (i % SSM_GROUPS, 0, 0))
    return pl.pallas_call(
        _s5_y_kernel,
        out_shape=jax.ShapeDtypeStruct((bg, kc, cw), BF16),
        grid=(bg,),
        in_specs=[blk(), blk(), wsp(), wsp()],
        out_specs=blk(),
        compiler_params=_cp(("parallel",)),
        name="s5_outputs",
    )(u, h, w_t, w_c)


def _glu_kernel(y_ref, w_ref, o_ref):
    z = _dot(y_ref[...], w_ref[...])
    half = z.shape[1] // 2
    o_ref[...] = (z[:, :half] * jax.nn.sigmoid(z[:, half:])).astype(o_ref.dtype)


def glu_matmul(y, w, tm):
    t, kdim = y.shape
    return pl.pallas_call(
        _glu_kernel,
        out_shape=jax.ShapeDtypeStruct((t, w.shape[1] // 2), BF16),
        grid=(t // tm,),
        in_specs=[pl.BlockSpec((tm, kdim), lambda i: (i, 0)), _const_spec(w.shape)],
        out_specs=pl.BlockSpec((tm, w.shape[1] // 2), lambda i: (i, 0)),
        compiler_params=_cp(("parallel",)),
        name="glu_matmul",
    )(y, w)


def s5_mixer(u_tok, tables, w_glu, n_batch, seq, ctx_len):
    w_t, w_z, w_c, a_re, a_im = tables
    tc = SSM_CHUNK
    n = seq + ctx_len
    kc, k_lat, k_ctx = n // tc, seq // tc, ctx_len // tc
    u_seq = jnp.concatenate([u_tok[:n_batch * seq].reshape(n_batch, seq, MIX_W),
                             u_tok[n_batch * seq:].reshape(n_batch, ctx_len, MIX_W)], axis=1)
    u_ch = u_seq.reshape(n_batch, kc, tc, SSM_GROUPS, SSM_CH).transpose(0, 3, 1, 2, 4).reshape(n_batch * SSM_GROUPS, kc, tc * SSM_CH)
    z = s5_chunk_sums(u_ch, w_z)
    h = s5_state_scan(z.transpose(1, 0, 2), jnp.tile(a_re, (n_batch, 1)), jnp.tile(a_im, (n_batch, 1)), k_lat, k_ctx)
    y = s5_outputs(u_ch, h.transpose(1, 0, 2), w_t, w_c)
    y = y.reshape(n_batch, SSM_GROUPS, kc, tc, SSM_CH).transpose(0, 2, 3, 1, 4).reshape(n_batch, n, MIX_W)
    y_tok = jnp.concatenate([y[:, :seq].reshape(n_batch * seq, MIX_W), y[:, seq:].reshape(n_batch * ctx_len, MIX_W)], axis=0)
    return glu_matmul(y_tok, w_glu, 512)


def _chan_dft_kernel(x_ref, w_ref, yc_ref, ys_ref):
    x = x_ref[...]
    for g in range(FNO_GROUPS):
        y = _dot(x[:, FNO_CH * g:FNO_CH * (g + 1)], w_ref[...])
        yc_ref[:, FNO_CH * g:FNO_CH * (g + 1)] = y[:, :FNO_CH].astype(BF16)
        ys_ref[:, FNO_CH * g:FNO_CH * (g + 1)] = y[:, FNO_CH:].astype(BF16)


def chan_dft(x, w, tm):
    t, cw = x.shape
    return pl.pallas_call(
        _chan_dft_kernel,
        out_shape=[jax.ShapeDtypeStruct((t, cw), BF16)] * 2,
        grid=(t // tm,),
        in_specs=[pl.BlockSpec((tm, cw), lambda i: (i, 0)), _const_spec(w.shape)],
        out_specs=[pl.BlockSpec((tm, cw), lambda i: (i, 0))] * 2,
        compiler_params=_cp(("parallel",)),
        name="chan_dft",
    )(x, w)


def _pos_dft_kernel(c_ref, s_ref, yc_ref, ys_ref, o_ref, acc_ref, *, n_k):
    j = pl.program_id(2)

    @pl.when(j == 0)
    def _():
        acc_ref[...] = jnp.zeros_like(acc_ref)

    acc_ref[...] += _dot(c_ref[...], yc_ref[0]) - _dot(s_ref[...], ys_ref[0])

    @pl.when(j == n_k - 1)
    def _():
        o_ref[0] = acc_ref[...].astype(o_ref.dtype)


def pos_dft(cos_t, sin_t, yc, ys, tm, tk):
    nb, n, w = yc.shape
    n_k = n // tk
    return pl.pallas_call(
        functools.partial(_pos_dft_kernel, n_k=n_k),
        out_shape=jax.ShapeDtypeStruct((nb, n, w), BF16),
        grid=(nb, n // tm, n_k),
        in_specs=[pl.BlockSpec((tm, tk), lambda b, i, j: (i, j)), pl.BlockSpec((tm, tk), lambda b, i, j: (i, j)),
                  pl.BlockSpec((1, tk, w), lambda b, i, j: (b, j, 0)), pl.BlockSpec((1, tk, w), lambda b, i, j: (b, j, 0))],
        out_specs=pl.BlockSpec((1, tm, w), lambda b, i, j: (b, i, 0)),
        scratch_shapes=[pltpu.VMEM((tm, w), F32)],
        compiler_params=_cp(("parallel", "parallel", "arbitrary")),
        name="pos_dft",
    )(cos_t, sin_t, yc, ys)


def dft_tables(n):
    idx = (jnp.arange(n, dtype=jnp.int32)[:, None] * jnp.arange(n, dtype=jnp.int32)[None, :]) % n
    ang = idx.astype(F32) * (2.0 * math.pi / n)
    return jnp.cos(ang).astype(BF16), jnp.sin(ang).astype(BF16)


def chan_table(n_pos):
    c = jnp.arange(FNO_CH, dtype=jnp.int32)
    ang = ((c[:, None] * c[None, :]) % FNO_CH).astype(F32) * (2.0 * math.pi / FNO_CH)
    scale = 1.0 / math.sqrt(n_pos * FNO_CH)
    return (jnp.concatenate([jnp.cos(ang), jnp.sin(ang)], axis=1) * scale).astype(BF16)


def fourier_mixer(x_seq, pos_tabs, tm, tk):
    nb, n, w = x_seq.shape
    yc, ys = chan_dft(x_seq.reshape(nb * n, w), chan_table(n), min(512, n))
    return pos_dft(pos_tabs[0], pos_tabs[1], yc.reshape(nb, n, w), ys.reshape(nb, n, w), tm, tk)


def _route(logits_t, bias_col):
    score = jax.nn.sigmoid(logits_t)
    sel = score + bias_col
    sa, sb, sc, sd = (sel[8 * i:8 * (i + 1)] for i in range(EXPERTS_PER_GROUP))
    ra, rb, rc, rd = (score[8 * i:8 * (i + 1)] for i in range(EXPERTS_PER_GROUP))
    m1, n1 = jnp.maximum(sa, sb), jnp.minimum(sa, sb)
    m2, n2 = jnp.maximum(sc, sd), jnp.minimum(sc, sd)
    gsum = jnp.maximum(m1, m2) + jnp.maximum(jnp.minimum(m1, m2), jnp.maximum(n1, n2))
    rows = lax.broadcasted_iota(jnp.int32, gsum.shape, 0)
    neg = jnp.float32(-jnp.inf)
    gsum = jnp.where(rows < N_EXPERT_GROUPS, gsum, neg)
    best = jnp.max(gsum, axis=0, keepdims=True)
    grp_f = jnp.min(jnp.where(gsum == best, rows.astype(F32), 8.0), axis=0, keepdims=True)
    grp = grp_f.astype(jnp.int32)
    hot = rows == grp
    pick = lambda v: jnp.sum(jnp.where(hot, v, 0.0), axis=0, keepdims=True)
    a, b, c, d = pick(sa), pick(sb), pick(sc), pick(sd)
    wa, wb, wc, wd = pick(ra), pick(rb), pick(rc), pick(rd)

    def first_max(va, vb, vc, vd):
        m = jnp.maximum(jnp.maximum(va, vb), jnp.maximum(vc, vd))
        return jnp.where(va == m, 0, jnp.where(vb == m, 1, jnp.where(vc == m, 2, 3)))

    i1 = first_max(a, b, c, d)
    i2 = first_max(jnp.where(i1 == 0, neg, a), jnp.where(i1 == 1, neg, b), jnp.where(i1 == 2, neg, c), jnp.where(i1 == 3, neg, d))
    gate = lambda i: jnp.where(i == 0, wa, jnp.where(i == 1, wb, jnp.where(i == 2, wc, wd)))
    g1, g2 = gate(i1), gate(i2)
    den = g1 + g2
    return grp * EXPERTS_PER_GROUP + i1, grp * EXPERTS_PER_GROUP + i2, g1 / den, g2 / den


def _outproj_kernel(pa_ref, pb_ref, pc_ref, pd_ref, x_ref, w_ref, g1_ref, gam_ref, sh_ref, sc_ref, rw_ref, rb_ref,
                    xo_ref, h_ref, e_ref, gt_ref):
    acc = _dot(pa_ref[...], w_ref[0:MIX_W])
    for i, p_ref in enumerate((pb_ref, pc_ref, pd_ref), start=1):
        acc += _dot(p_ref[...], w_ref[MIX_W * i:MIX_W * (i + 1)])
    x = x_ref[...] + g1_ref[0] * acc
    xo_ref[...] = x
    h = _rms(x, gam_ref[...]) * (1.0 + sc_ref[0]) + sh_ref[0]
    h_ref[...] = h.astype(BF16)
    logits = _dot3(h, rw_ref[...])
    e1, e2, g1, g2 = _route(logits.T[:32], rb_ref[...])
    rows = lax.broadcasted_iota(jnp.int32, e_ref.shape, 0)
    e_ref[...] = jnp.where(rows == 0, e1, jnp.where(rows == 1, e2, 0))
    gt_ref[...] = jnp.where(rows == 0, g1, jnp.where(rows == 1, g2, 0.0))


def out_projection(parts, tok, w_out, gate1, gamma, shift, scale, rw, rb, n_rows, lat_tiles_per_batch, n_batch, tm):
    d = tok.shape[1]
    mod_idx = lambda i: (jnp.minimum(i // lat_tiles_per_batch, n_batch), 0, 0)
    row = lambda w: pl.BlockSpec((tm, w), lambda i: (i, 0))
    mod = lambda: pl.BlockSpec((1, 1, d), mod_idx)
    return pl.pallas_call(
        _outproj_kernel,
        out_shape=[jax.ShapeDtypeStruct((n_rows, d), F32), jax.ShapeDtypeStruct((n_rows, d), BF16),
                   jax.ShapeDtypeStruct((8, n_rows), jnp.int32), jax.ShapeDtypeStruct((8, n_rows), F32)],
        grid=(n_rows // tm,),
        in_specs=[row(MIX_W), row(MIX_W), row(MIX_W), row(MIX_W), row(d),
                  pl.BlockSpec(w_out.shape, lambda i: (0, 0), pipeline_mode=pl.Buffered(1)),
                  mod(), _const_spec((1, d)), mod(), mod(), _const_spec(rw.shape), _const_spec(rb.shape)],
        out_specs=[row(d), row(d), pl.BlockSpec((8, tm), lambda i: (0, i)), pl.BlockSpec((8, tm), lambda i: (0, i))],
        compiler_params=_cp(("parallel",)),
        name="out_projection",
    )(*parts, tok, w_out, gate1, gamma.reshape(1, d), shift, scale, rw, rb)


def _moe_kernel(be_ref, nu_ref, x_ref, g_ref, wg_ref, wu_ref, wd_ref, y_ref):
    i = pl.program_id(0)

    @pl.when(i < nu_ref[0])
    def _():
        x = x_ref[...]
        a = (jax.nn.silu(_dot(x, wg_ref[0])) * _dot(x, wu_ref[0])).astype(BF16)
        y_ref[...] = _dot(a, wd_ref[0]) * g_ref[...]

    @pl.when(i >= nu_ref[0])
    def _():
        y_ref[...] = jnp.zeros_like(y_ref)


def moe_experts(xb, row_gate, block_expert, n_used, wg, wu, wd):
    n_rows, d = xb.shape
    n_blocks = n_rows // MOE_BLOCK
    de = wg.shape[2]
    grid_spec = pltpu.PrefetchScalarGridSpec(
        num_scalar_prefetch=2,
        grid=(n_blocks,),
        in_specs=[pl.BlockSpec((MOE_BLOCK, d), lambda i, be, nu: (i, 0)), pl.BlockSpec((MOE_BLOCK, 1), lambda i, be, nu: (i, 0)),
                  pl.BlockSpec((1, d, de), lambda i, be, nu: (be[i], 0, 0)), pl.BlockSpec((1, d, de), lambda i, be, nu: (be[i], 0, 0)),
                  pl.BlockSpec((1, de, d), lambda i, be, nu: (be[i], 0, 0))],
        out_specs=pl.BlockSpec((MOE_BLOCK, d), lambda i, be, nu: (i, 0)),
    )
    return pl.pallas_call(
        _moe_kernel,
        out_shape=jax.ShapeDtypeStruct((n_rows, d), F32),
        grid_spec=grid_spec,
        compiler_params=_cp(("arbitrary",)),
        name="moe_experts",
    )(block_expert, n_used, xb, row_gate, wg, wu, wd)


def routed_moe(h, eid, gate, wg, wu, wd):
    t = h.shape[0]
    bm = MOE_BLOCK
    flat_e = eid[:2].reshape(-1)
    flat_g = gate[:2].reshape(-1)
    flat_tok = jnp.tile(jnp.arange(t, dtype=jnp.int32), 2)
    hot = (flat_e[:, None] == jnp.arange(N_EXPERTS, dtype=jnp.int32)[None, :]).astype(jnp.int32)
    csum = jnp.cumsum(hot, axis=0)
    counts = csum[-1]
    padded = (counts + bm - 1) // bm * bm
    pad_end = jnp.cumsum(padded)
    pad_start = pad_end - padded
    dest = jnp.sum(hot * (pad_start[None, :] + csum - 1), axis=1)
    n_blocks = (2 * t + N_EXPERTS * (bm - 1) + bm - 1) // bm
    n_rows = n_blocks * bm
    row_tok = jnp.zeros((n_rows,), jnp.int32).at[dest].set(flat_tok)
    row_gate = jnp.zeros((n_rows,), F32).at[dest].set(flat_g)
    block_expert = jnp.minimum(jnp.searchsorted(pad_end, jnp.arange(n_blocks, dtype=jnp.int32) * bm, side='right'),
                               N_EXPERTS - 1).astype(jnp.int32)
    n_used = (pad_end[-1] // bm).astype(jnp.int32).reshape(1)
    yb = moe_experts(h[row_tok], row_gate.reshape(n_rows, 1), block_expert, n_used, wg, wu, wd)
    pos = dest.reshape(2, t)
    return yb[pos[0]] + yb[pos[1]]


def _residual_kernel(x_ref, y_ref, g_ref, o_ref):
    o_ref[...] = x_ref[...] + g_ref[0] * y_ref[...]


def _final_kernel(x_ref, y_ref, g_ref, gam_ref, o_ref):
    o_ref[...] = _rms(x_ref[...] + g_ref[0] * y_ref[...], gam_ref[...])


def gated_residual(x, y, gate, lat_tiles_per_batch, n_batch, tm, final_gamma=None):
    t, d = x.shape
    mod_idx = lambda i: (jnp.minimum(i // lat_tiles_per_batch, n_batch), 0, 0)
    row = lambda: pl.BlockSpec((tm, d), lambda i: (i, 0))
    in_specs = [row(), row(), pl.BlockSpec((1, 1, d), mod_idx)]
    args = [x, y, gate]
    body = _residual_kernel
    if final_gamma is not None:
        in_specs.append(_const_spec((1, d)))
        args.append(final_gamma.reshape(1, d))
        body = _final_kernel
    return pl.pallas_call(
        body,
        out_shape=jax.ShapeDtypeStruct((t, d), F32),
        grid=(t // tm,),
        in_specs=in_specs,
        out_specs=row(),
        compiler_params=_cp(("parallel",)),
        name="gated_residual",
    )(*args)


def _rope_tables(seq, ctx_len):
    pos = jnp.arange(seq)
    row = (pos // GRID_W).astype(F32)
    col = (pos % GRID_W).astype(F32)

    def tab(dim):
        quarter = dim // 4
        inv = ROPE_THETA ** (-jnp.arange(quarter, dtype=F32) / quarter)
        ang = jnp.concatenate([row[:, None] * inv, col[:, None] * inv], axis=-1)
        cos, sin = jnp.cos(ang), jnp.sin(ang)
        cos_f = jnp.concatenate([cos, cos], axis=-1)
        sin_f = jnp.concatenate([-sin, sin], axis=-1)
        cos_f = jnp.concatenate([cos_f, jnp.ones((ctx_len, dim), F32)], axis=0)
        sin_f = jnp.concatenate([sin_f, jnp.zeros((ctx_len, dim), F32)], axis=0)
        pad = LANE - dim
        return jnp.pad(cos_f, ((0, 0), (0, pad))), jnp.pad(sin_f, ((0, 0), (0, pad)))

    cm, sm = tab(MLA_ROPE)
    cg, sg = tab(GQA_HD)
    return cm, sm, cg, sg


def _half_swap(w, dim):
    return jnp.concatenate([w[..., dim // 2:], w[..., :dim // 2]], axis=-1)


def _layout_w_in(w):
    d = w.shape[0]
    k_rot = w[:, MLA_Q_RANK + MLA_KV_RANK:MLA_IN]
    z = jnp.zeros((d, LANE - MLA_ROPE), w.dtype)
    return jnp.concatenate([w[:, :MLA_Q_RANK + MLA_KV_RANK], k_rot, z, _half_swap(k_rot, MLA_ROPE), z, w[:, OFF_B:]], axis=1).astype(BF16)


def _layout_w_uq(w):
    r = w.shape[0]
    wh = w.reshape(r, MLA_HEADS, MLA_NOPE + MLA_ROPE)
    rope = wh[..., MLA_NOPE:]
    z = jnp.zeros((r, MLA_HEADS, LANE - MLA_ROPE), w.dtype)
    main = jnp.concatenate([wh[..., :MLA_NOPE], rope, z], axis=-1).reshape(r, MLA_HEADS * MLA_DQ)
    rot = jnp.concatenate([_half_swap(rope, MLA_ROPE), z], axis=-1).reshape(r, MLA_HEADS * LANE)
    return jnp.concatenate([main, rot], axis=1).astype(BF16)


def _layout_router(router_w, router_bias):
    d = router_w.shape[0]
    w4 = router_w.reshape(d, N_EXPERT_GROUPS, EXPERTS_PER_GROUP).transpose(0, 2, 1)
    w = jnp.pad(w4, ((0, 0), (0, 0), (0, 8 - N_EXPERT_GROUPS))).reshape(d, 8 * EXPERTS_PER_GROUP)
    w = jnp.pad(w, ((0, 0), (0, LANE - 8 * EXPERTS_PER_GROUP))).astype(F32)
    b4 = router_bias.astype(F32).reshape(N_EXPERT_GROUPS, EXPERTS_PER_GROUP).T
    b = jnp.pad(b4, ((0, 0), (0, 8 - N_EXPERT_GROUPS))).reshape(8 * EXPERTS_PER_GROUP, 1)
    return w, b


def kernel(x, c, ctx, c_ctx, ada_w, ada_b, norm1_g, norm2_g, w_in, mla_q_norm_g, mla_w_uq, mla_kv_norm_g, mla_w_ukv, gqa_q_norm_g, gqa_k_norm_g, ssm_lam_re, ssm_lam_im, ssm_log_dt, ssm_b_re, ssm_b_im, ssm_c_re, ssm_c_im, ssm_d, ssm_w_glu, w_out, router_w, router_bias, moe_w_gate, moe_w_up, moe_w_down, final_norm_g):
    nb, seq, d = x.shape
    ctx_len = ctx.shape[1]
    depth = ada_w.shape[0]
    n_lat, n_ctx = nb * seq, nb * ctx_len
    tm = min(512, n_ctx)
    lat_tiles = seq // tm
    tq = min(512, seq)
    tk_lat = 1280 if (seq + ctx_len) % 1280 == 0 else ctx_len
    in_widths = (768, GQA_IN, MIX_W, MIX_W)

    tok = jnp.concatenate([x.reshape(n_lat, d), ctx.reshape(n_ctx, d)], axis=0)
    silu = jnp.concatenate([jax.nn.silu(c), jnp.broadcast_to(jax.nn.silu(c_ctx)[None], (8 - nb, d))], axis=0)
    rope_tabs = _rope_tables(seq, ctx_len)
    rw, rb = _layout_router(router_w, router_bias)
    dft_lat = dft_tables(seq)
    dft_ctx = dft_tables(ctx_len)

    for l in range(depth):
        need_ctx = l < depth - 1
        mod = mod_vectors(silu, ada_w[l], ada_b[l])[:nb + 1].reshape(nb + 1, 1, 6, d)
        sh1, sc1, g1, sh2, sc2, g2 = (mod[:, :, i] for i in range(6))

        pa, pb, pc, pd = in_projection(tok, norm1_g[l], sh1, sc1, _layout_w_in(w_in[l]), in_widths, lat_tiles, nb, tm)

        qm, km, vm, qg, kg, vg = qkv_prepare(pa, pb, rope_tabs, mla_q_norm_g[l], mla_kv_norm_g[l], gqa_q_norm_g[l],
                                             gqa_k_norm_g[l], _layout_w_uq(mla_w_uq[l]), mla_w_ukv[l].astype(BF16),
                                             nb, seq, ctx_len, min(256, ctx_len))
        qm5 = qm.reshape(nb, MLA_HEADS, 1, seq + ctx_len, MLA_DQ)
        out_a = flash_attention(qm5, km, vm, 0, seq, seq + ctx_len, tq, tk_lat).reshape(n_lat, MIX_W)
        out_b = flash_attention(qg, kg, vg, 0, seq, seq + ctx_len, tq, tk_lat).reshape(n_lat, MIX_W)

        s5_tabs = s5_tables(ssm_lam_re[l], ssm_lam_im[l], ssm_log_dt[l], ssm_b_re[l], ssm_b_im[l], ssm_c_re[l], ssm_c_im[l], ssm_d[l])
        out_c = s5_mixer(pc, s5_tabs, ssm_w_glu[l].astype(BF16), nb, seq, ctx_len)

        out_d = fourier_mixer(pd[:n_lat].reshape(nb, seq, MIX_W), dft_lat, min(1024, seq), min(2048, seq)).reshape(n_lat, MIX_W)

        if need_ctx:
            ctx_a = flash_attention(qm5, km, vm, seq, ctx_len, ctx_len, ctx_len, ctx_len).reshape(n_ctx, MIX_W)
            ctx_b = flash_attention(qg, kg, vg, seq, ctx_len, ctx_len, ctx_len, ctx_len).reshape(n_ctx, MIX_W)
            ctx_d = fourier_mixer(pd[n_lat:].reshape(nb, ctx_len, MIX_W), dft_ctx, ctx_len, ctx_len).reshape(n_ctx, MIX_W)
            parts = [jnp.concatenate([out_a, ctx_a], axis=0), jnp.concatenate([out_b, ctx_b], axis=0), out_c,
                     jnp.concatenate([out_d, ctx_d], axis=0)]
            n_rows = n_lat + n_ctx
        else:
            parts = [out_a, out_b, out_c, out_d]
            n_rows = n_lat

        tok, h2, eid, gate = out_projection(parts, tok, w_out[l].astype(BF16), g1, norm2_g[l], sh2, sc2, rw, rb,
                                            n_rows, 2 * lat_tiles, nb, tm // 2)
        y = routed_moe(h2, eid, gate, moe_w_gate[l].astype(BF16), moe_w_up[l].astype(BF16), moe_w_down[l].astype(BF16))
        tok = gated_residual(tok, y, g2, lat_tiles, nb, tm, None if need_ctx else final_norm_g)

    return tok[:n_lat].reshape(nb, seq, d)
```

```python
import functools
import math

import jax
import jax.numpy as jnp
from jax import lax
from jax.experimental import pallas as pl
from jax.experimental.pallas import tpu as pltpu

F32 = jnp.float32
BF16 = jnp.bfloat16

D_MODEL = 2048
GRID_W = 64
ROPE_THETA = 10000.0
RMS_EPS = 1e-6
MIX_W = D_MODEL // 4
MLA_HEADS = 4
MLA_NOPE = 128
MLA_V = 128
MLA_ROPE = 64
MLA_Q_RANK = 384
MLA_KV_RANK = 128
MLA_IN = MLA_Q_RANK + MLA_KV_RANK + MLA_ROPE
MLA_DQ = 256
GQA_HEADS = 4
GQA_KV_HEADS = 2
GQA_HD = 128
GQA_IN = (GQA_HEADS + 2 * GQA_KV_HEADS) * GQA_HD
SSM_GROUPS = 32
SSM_CH = 16
SSM_STATE = 64
SSM_CHUNK = 16
FNO_GROUPS = 4
FNO_CH = 128
OFF_B = MLA_IN
OFF_C = OFF_B + GQA_IN
OFF_D = OFF_C + MIX_W
N_EXPERTS = 16
N_EXPERT_GROUPS = 4
EXPERTS_PER_GROUP = 4
D_EXPERT = D_MODEL // 2
MOE_BLOCK = 256
LANE = 128
MXU_W = 256
LOG2E = 1.4426950408889634
VMEM_LIMIT = 52 * 1024 * 1024


def _cp(sem, vmem=VMEM_LIMIT):
    return pltpu.CompilerParams(dimension_semantics=sem, vmem_limit_bytes=vmem)


def _const_spec(shape):
    nd = len(shape)
    return pl.BlockSpec(shape, lambda *_: (0,) * nd)


def _split_bf16(a):
    hi = a.astype(BF16)
    lo = (a - hi.astype(F32)).astype(BF16)
    return hi, lo


def _dot(a, b):
    return jnp.dot(a, b, preferred_element_type=F32)


def _dot3(a, b):
    ah, al = _split_bf16(a)
    bh, bl = _split_bf16(b)
    return _dot(ah, bh) + _dot(al, bh) + _dot(ah, bl)


def _rms(x, g):
    return x * lax.rsqrt(jnp.mean(x * x, axis=-1, keepdims=True) + RMS_EPS) * g


def _mod_kernel(s_ref, w_ref, b_ref, o_ref):
    o_ref[...] = _dot3(s_ref[...], w_ref[...]) + b_ref[...]


def mod_vectors(s8, w, b):
    k, n = w.shape
    tn = 1024
    return pl.pallas_call(
        _mod_kernel,
        out_shape=jax.ShapeDtypeStruct((8, n), F32),
        grid=(n // tn,),
        in_specs=[_const_spec((8, k)), pl.BlockSpec((k, tn), lambda j: (0, j)), pl.BlockSpec((1, tn), lambda j: (0, j))],
        out_specs=pl.BlockSpec((8, tn), lambda j: (0, j)),
        compiler_params=_cp(("parallel",)),
        name="mod_vectors",
    )(s8, w, b.reshape(1, n))


def _inproj_kernel(x_ref, g_ref, sh_ref, sc_ref, w_ref, *o_refs, widths):
    h = _rms(x_ref[...], g_ref[...])
    hb = (h * (1.0 + sc_ref[0]) + sh_ref[0]).astype(BF16)
    off = 0
    for o_ref, wd in zip(o_refs, widths):
        o_ref[...] = _dot(hb, w_ref[:, off:off + wd]).astype(o_ref.dtype)
        off += wd


def in_projection(tok, gamma, shift, scale, w, widths, lat_tiles_per_batch, n_batch, tm):
    t, d = tok.shape
    mod_idx = lambda i: (jnp.minimum(i // lat_tiles_per_batch, n_batch), 0, 0)
    return pl.pallas_call(
        functools.partial(_inproj_kernel, widths=widths),
        out_shape=[jax.ShapeDtypeStruct((t, wd), BF16) for wd in widths],
        grid=(t // tm,),
        in_specs=[pl.BlockSpec((tm, d), lambda i: (i, 0)), _const_spec((1, d)),
                  pl.BlockSpec((1, 1, d), mod_idx), pl.BlockSpec((1, 1, d), mod_idx),
                  pl.BlockSpec(w.shape, lambda i: (0, 0), pipeline_mode=pl.Buffered(1))],
        out_specs=[pl.BlockSpec((tm, wd), lambda i: (i, 0)) for wd in widths],
        compiler_params=_cp(("parallel",)),
        name="in_projection",
    )(tok, gamma.reshape(1, d), shift, scale, w)


def _qkv_kernel(a_ref, b_ref, cm_ref, sm_ref, cg_ref, sg_ref, gq_ref, gkv_ref, ggq_ref, ggk_ref, wuq_ref, wukv_ref,
                qm_ref, km_ref, vm_ref, qg_ref, kg_ref, vg_ref):
    a = a_ref[...].astype(F32)
    cm, sm = cm_ref[...], sm_ref[...]
    mla_scale = LOG2E * (MLA_NOPE + MLA_ROPE) ** -0.5
    qn = _rms(a[:, :MLA_Q_RANK], gq_ref[...]).astype(BF16)
    q_all = _dot(qn, wuq_ref[...])
    rot0 = MLA_HEADS * MLA_DQ
    for h in range(MLA_HEADS):
        nope = q_all[:, MLA_DQ * h:MLA_DQ * h + LANE]
        rp = q_all[:, MLA_DQ * h + LANE:MLA_DQ * (h + 1)]
        rr = q_all[:, rot0 + LANE * h:rot0 + LANE * (h + 1)]
        qm_ref[0, h, :, :LANE] = (nope * mla_scale).astype(BF16)
        qm_ref[0, h, :, LANE:] = ((rp * cm + rr * sm) * mla_scale).astype(BF16)
    kvn = _rms(a[:, MLA_Q_RANK:MLA_Q_RANK + MLA_KV_RANK], gkv_ref[...]).astype(BF16)
    kv = _dot(kvn, wukv_ref[...])
    kr = (a[:, 512:640] * cm + a[:, 640:768] * sm).astype(BF16)
    for h in range(MLA_HEADS):
        km_ref[0, h, :, :LANE] = kv[:, 256 * h:256 * h + LANE].astype(BF16)
        km_ref[0, h, :, LANE:] = kr
        vm_ref[0, h] = kv[:, 256 * h + LANE:256 * (h + 1)].T.astype(BF16)
    b = b_ref[...].astype(F32)
    cg, sg = cg_ref[...], sg_ref[...]
    gqa_scale = LOG2E * GQA_HD ** -0.5
    grp = GQA_HEADS // GQA_KV_HEADS
    for h in range(GQA_HEADS):
        qh = _rms(b[:, GQA_HD * h:GQA_HD * (h + 1)], ggq_ref[...])
        qh = qh * cg + pltpu.roll(qh, GQA_HD // 2, 1) * sg
        qg_ref[0, h // grp, h % grp] = (qh * gqa_scale).astype(BF16)
    k0 = GQA_HEADS * GQA_HD
    v0 = k0 + GQA_KV_HEADS * GQA_HD
    for h in range(GQA_KV_HEADS):
        kh = _rms(b[:, k0 + GQA_HD * h:k0 + GQA_HD * (h + 1)], ggk_ref[...])
        kg_ref[0, h] = (kh * cg + pltpu.roll(kh, GQA_HD // 2, 1) * sg).astype(BF16)
        vg_ref[0, h] = b[:, v0 + GQA_HD * h:v0 + GQA_HD * (h + 1)].T.astype(BF16)

def qkv_prepare(pa, pb, tabs, g_q, g_kv, g_gq, g_gk, w_uq, w_ukv, n_batch, seq, ctx_len, tm):
    t = pa.shape[0]
    n = seq + ctx_len
    lat_tiles = seq // tm
    ctx_tiles = ctx_len // tm
    n_lat = n_batch * lat_tiles

    def bidx(i):
        j = i - n_lat
        bi = jnp.where(i < n_lat, i // lat_tiles, j // ctx_tiles)
        blk = jnp.where(i < n_lat, i % lat_tiles, lat_tiles + j % ctx_tiles)
        return bi, blk

    def o4(i):
        bi, blk = bidx(i)
        return (bi, 0, blk, 0)

    def o4t(i):
        bi, blk = bidx(i)
        return (bi, 0, 0, blk)

    def o5(i):
        bi, blk = bidx(i)
        return (bi, 0, 0, blk, 0)

    tab_spec = pl.BlockSpec((tm, LANE), lambda i: (bidx(i)[1], 0))
    vec = lambda w: _const_spec((1, w))
    return pl.pallas_call(
        _qkv_kernel,
        out_shape=[jax.ShapeDtypeStruct((n_batch, MLA_HEADS, n, MLA_DQ), BF16),
                   jax.ShapeDtypeStruct((n_batch, MLA_HEADS, n, MLA_DQ), BF16),
                   jax.ShapeDtypeStruct((n_batch, MLA_HEADS, MLA_V, n), BF16),
                   jax.ShapeDtypeStruct((n_batch, GQA_KV_HEADS, GQA_HEADS // GQA_KV_HEADS, n, GQA_HD), BF16),
                   jax.ShapeDtypeStruct((n_batch, GQA_KV_HEADS, n, GQA_HD), BF16),
                   jax.ShapeDtypeStruct((n_batch, GQA_KV_HEADS, GQA_HD, n), BF16)],
        grid=(t // tm,),
        in_specs=[pl.BlockSpec((tm, pa.shape[1]), lambda i: (i, 0)), pl.BlockSpec((tm, pb.shape[1]), lambda i: (i, 0)),
                  tab_spec, tab_spec, tab_spec, tab_spec,
                  vec(MLA_Q_RANK), vec(MLA_KV_RANK), vec(GQA_HD), vec(GQA_HD),
                  _const_spec(w_uq.shape), _const_spec(w_ukv.shape)],
        out_specs=[pl.BlockSpec((1, MLA_HEADS, tm, MLA_DQ), o4), pl.BlockSpec((1, MLA_HEADS, tm, MLA_DQ), o4),
                   pl.BlockSpec((1, MLA_HEADS, MLA_V, tm), o4t),
                   pl.BlockSpec((1, GQA_KV_HEADS, GQA_HEADS // GQA_KV_HEADS, tm, GQA_HD), o5),
                   pl.BlockSpec((1, GQA_KV_HEADS, tm, GQA_HD), o4), pl.BlockSpec((1, GQA_KV_HEADS, GQA_HD, tm), o4t)],
        compiler_params=_cp(("parallel",)),
        name="qkv_prepare",
    )(pa, pb, *tabs, g_q.reshape(1, -1), g_kv.reshape(1, -1), g_gq.reshape(1, -1), g_gk.reshape(1, -1), w_uq, w_ukv)


def _dot_nt(a, b):
    return lax.dot_general(a, b, (((1,), (1,)), ((), ())), preferred_element_type=F32)


def _softmax_cols(s):
    m = jnp.max(s, axis=0, keepdims=True)
    p = jnp.exp2(s - m)
    return m, p, jnp.sum(p, axis=0, keepdims=True)


def _store_heads(o_ref, o_t, grp, tq, dv):
    o = o_t.T
    for g in range(grp):
        o_ref[0, :, dv * g:dv * (g + 1)] = o[tq * g:tq * (g + 1)].astype(o_ref.dtype)


def _flash_kernel(q_ref, k_ref, vt_ref, kc_ref, vtc_ref, o_ref, m_sc, l_sc, acc_sc, *, grp, tq, dv, n_kv):
    j = pl.program_id(3)
    rows = grp * tq
    q = q_ref[0, 0].reshape(rows, q_ref.shape[-1])

    @pl.when(j == 0)
    def _():
        m, p, l = _softmax_cols(_dot_nt(kc_ref[0, 0], q))
        m_sc[...] = m
        l_sc[...] = l
        acc_sc[...] = _dot(vtc_ref[0, 0], p.astype(BF16))

    k = k_ref[0, 0]
    vt = vt_ref[0, 0]
    slabs = [slice(MXU_W * r, MXU_W * (r + 1)) for r in range(rows // MXU_W)]
    scores = [_dot_nt(k, q[sl]) for sl in slabs]
    for sl, s in zip(slabs, scores):
        m_prev = m_sc[:, sl]
        m_new = jnp.maximum(m_prev, jnp.max(s, axis=0, keepdims=True))
        alpha = jnp.exp2(m_prev - m_new)
        p = jnp.exp2(s - m_new)
        l_sc[:, sl] = alpha * l_sc[:, sl] + jnp.sum(p, axis=0, keepdims=True)
        acc_sc[:, sl] = alpha * acc_sc[:, sl] + _dot(vt, p.astype(BF16))
        m_sc[:, sl] = m_new

    @pl.when(j == n_kv - 1)
    def _():
        _store_heads(o_ref, acc_sc[...] / l_sc[...], grp, tq, dv)


def flash_attention(q, k, vt, seq, ctx_len, tq, tk):
    nb, hk, grp, n, dq = q.shape
    dv = vt.shape[-2]
    n_kv = seq // tk
    cb = seq // ctx_len
    rows = grp * tq
    return pl.pallas_call(
        functools.partial(_flash_kernel, grp=grp, tq=tq, dv=dv, n_kv=n_kv),
        out_shape=jax.ShapeDtypeStruct((nb, seq, hk * grp * dv), BF16),
        grid=(nb, hk, seq // tq, n_kv),
        in_specs=[pl.BlockSpec((1, 1, grp, tq, dq), lambda b, h, i, j: (b, h, 0, i, 0)),
                  pl.BlockSpec((1, 1, tk, dq), lambda b, h, i, j: (b, h, j, 0)),
                  pl.BlockSpec((1, 1, dv, tk), lambda b, h, i, j: (b, h, 0, j)),
                  pl.BlockSpec((1, 1, ctx_len, dq), lambda b, h, i, j: (b, h, cb, 0)),
                  pl.BlockSpec((1, 1, dv, ctx_len), lambda b, h, i, j: (b, h, 0, cb))],
        out_specs=pl.BlockSpec((1, tq, grp * dv), lambda b, h, i, j: (b, i, h)),
        scratch_shapes=[pltpu.VMEM((1, rows), F32), pltpu.VMEM((1, rows), F32), pltpu.VMEM((dv, rows), F32)],
        compiler_params=_cp(("parallel", "parallel", "parallel", "arbitrary")),
        name="flash_attention",
    )(q, k, vt, k, vt)


def _ctx_attn_kernel(q_ref, kc_ref, vtc_ref, o_ref, *, grp, tq, dv):
    q = q_ref[0, 0].reshape(grp * tq, q_ref.shape[-1])
    _, p, l = _softmax_cols(_dot_nt(kc_ref[0, 0], q))
    _store_heads(o_ref, _dot(vtc_ref[0, 0], p.astype(BF16)) / l, grp, tq, dv)


def context_attention(q, k, vt, seq, ctx_len):
    nb, hk, grp, n, dq = q.shape
    dv = vt.shape[-2]
    cb = seq // ctx_len
    return pl.pallas_call(
        functools.partial(_ctx_attn_kernel, grp=grp, tq=ctx_len, dv=dv),
        out_shape=jax.ShapeDtypeStruct((nb, ctx_len, hk * grp * dv), BF16),
        grid=(nb, hk),
        in_specs=[pl.BlockSpec((1, 1, grp, ctx_len, dq), lambda b, h: (b, h, 0, cb, 0)),
                  pl.BlockSpec((1, 1, ctx_len, dq), lambda b, h: (b, h, cb, 0)),
                  pl.BlockSpec((1, 1, dv, ctx_len), lambda b, h: (b, h, 0, cb))],
        out_specs=pl.BlockSpec((1, ctx_len, grp * dv), lambda b, h: (b, 0, h)),
        compiler_params=_cp(("parallel", "parallel")),
        name="context_attention",
    )(q, k, vt)


def s5_tables(lam_re, lam_im, log_dt, b_re, b_im, c_re, c_im, d):
    hp = lax.Precision.HIGHEST
    tc = SSM_CHUNK
    lam = lax.complex(lam_re.astype(F32), lam_im.astype(F32))
    dt = jnp.exp(log_dt.astype(F32))[..., None]
    lam_bar = jnp.exp(lam * dt)
    b_bar = ((lam_bar - 1.0) / lam)[..., None] * lax.complex(b_re.astype(F32), b_im.astype(F32))
    c_mat = lax.complex(c_re.astype(F32), c_im.astype(F32))
    steps = jnp.arange(tc + 1, dtype=F32)
    lam_pow = jnp.exp((lam * dt)[:, :, None, :] * steps[None, None, :, None])
    kern = jnp.einsum('dgip,dgtp,dgpj->dgtij', c_mat, lam_pow[:, :, :tc], b_bar, precision=hp).real
    s_idx = jnp.arange(tc)[:, None]
    t_idx = jnp.arange(tc)[None, :]
    kf = kern[0][:, jnp.clip(t_idx - s_idx, 0, tc - 1)] * (t_idx >= s_idx)[None, :, :, None, None]
    kb = kern[1][:, jnp.clip(s_idx - t_idx, 0, tc - 1)] * (s_idx >= t_idx)[None, :, :, None, None]
    dg = d.astype(F32).reshape(SSM_GROUPS, SSM_CH)
    skip = (s_idx == t_idx)[None, :, :, None, None] * (jnp.eye(SSM_CH, dtype=F32) * dg[:, :, None])[:, None, None]
    w_t = (kf + kb + skip).transpose(0, 1, 4, 2, 3).reshape(SSM_GROUPS, tc * SSM_CH, tc * SSM_CH)
    zf = lam_pow[0][:, tc - 1 - jnp.arange(tc)][..., None] * b_bar[0][:, None]
    zb = lam_pow[1][:, jnp.arange(tc)][..., None] * b_bar[1][:, None]
    to_rows = lambda z: z.transpose(0, 1, 3, 2).reshape(SSM_GROUPS, tc * SSM_CH, SSM_STATE)
    w_z = jnp.concatenate([to_rows(zf.real), to_rows(zb.real), to_rows(zf.imag), to_rows(zb.imag)], axis=-1)
    mf = c_mat[0][:, None] * lam_pow[0][:, 1 + jnp.arange(tc)][:, :, None, :]
    mb = c_mat[1][:, None] * lam_pow[1][:, tc - jnp.arange(tc)][:, :, None, :]
    to_cols = lambda m: m.transpose(0, 3, 1, 2).reshape(SSM_GROUPS, SSM_STATE, tc * SSM_CH)
    w_c = jnp.concatenate([to_cols(mf.real), to_cols(mb.real), -to_cols(mf.imag), -to_cols(mb.imag)], axis=1)
    a16 = lam_pow[:, :, tc]
    a_re = jnp.concatenate([a16[0].real, a16[1].real], axis=-1)
    a_im = jnp.concatenate([a16[0].imag, a16[1].imag], axis=-1)
    return w_t.astype(BF16), w_z.astype(BF16), w_c.astype(BF16), a_re, a_im


def _s5_z_kernel(u_ref, w_ref, z_ref):
    z_ref[0] = _dot(u_ref[0], w_ref[0])


def s5_chunk_sums(u, w_z):
    bg, kc, cw = u.shape
    return pl.pallas_call(
        _s5_z_kernel,
        out_shape=jax.ShapeDtypeStruct((bg, kc, cw), F32),
        grid=(bg,),
        in_specs=[pl.BlockSpec((1, kc, cw), lambda i: (i, 0, 0)), pl.BlockSpec((1, cw, cw), lambda i: (i % SSM_GROUPS, 0, 0))],
        out_specs=pl.BlockSpec((1, kc, cw), lambda i: (i, 0, 0)),
        compiler_params=_cp(("parallel",)),
        name="s5_chunk_sums",
    )(u, w_z)


def _s5_scan_kernel(z_ref, ar_ref, ai_ref, h_ref, *, k_lat, k_ctx):
    ar, ai = ar_ref[...], ai_ref[...]
    rows = ar.shape[0]
    fwd_lanes = lax.broadcasted_iota(jnp.int32, (rows, LANE), 1) < SSM_STATE

    def step(cf, cb, hr, hi):
        h_ref[cf, :, 0:SSM_STATE] = hr[:, :SSM_STATE]
        h_ref[cb, :, SSM_STATE:LANE] = hr[:, SSM_STATE:]
        h_ref[cf, :, LANE:LANE + SSM_STATE] = hi[:, :SSM_STATE]
        h_ref[cb, :, LANE + SSM_STATE:] = hi[:, SSM_STATE:]
        zf, zb = z_ref[cf], z_ref[cb]
        zr = jnp.where(fwd_lanes, zf[:, :LANE], zb[:, :LANE])
        zi = jnp.where(fwd_lanes, zf[:, LANE:], zb[:, LANE:])
        return ar * hr - ai * hi + zr, ar * hi + ai * hr + zi

    def ctx_body(n, c):
        return step(k_lat + n, k_lat + k_ctx - 1 - n, *c)

    def lat_body(n, c):
        return step(n, k_lat - 1 - n, *c)

    zero = jnp.zeros((rows, LANE), F32)
    carry = lax.fori_loop(0, k_ctx, ctx_body, (zero, zero))
    lax.fori_loop(0, k_lat, lat_body, carry)


def s5_state_scan(z, a_re, a_im, k_lat, k_ctx):
    kc, r, cw = z.shape
    tr = 8
    return pl.pallas_call(
        functools.partial(_s5_scan_kernel, k_lat=k_lat, k_ctx=k_ctx),
        out_shape=jax.ShapeDtypeStruct((kc, r, cw), F32),
        grid=(r // tr,),
        in_specs=[pl.BlockSpec((kc, tr, cw), lambda i: (0, i, 0)), pl.BlockSpec((tr, LANE), lambda i: (i, 0)),
                  pl.BlockSpec((tr, LANE), lambda i: (i, 0))],
        out_specs=pl.BlockSpec((kc, tr, cw), lambda i: (0, i, 0)),
        compiler_params=_cp(("parallel",)),
        name="s5_state_scan",
    )(z, a_re, a_im)


def _gelu_tanh(y):
    return 0.5 * y * (1.0 + jnp.tanh(math.sqrt(2.0 / math.pi) * (y + 0.044715 * (y * y * y))))


def _s5_y_kernel(u_ref, h_ref, wt_ref, wc_ref, y_ref):
    y = _dot(u_ref[0], wt_ref[0]) + _dot(h_ref[0].astype(BF16), wc_ref[0])
    y_ref[0] = _gelu_tanh(y).astype(y_ref.dtype)


def s5_outputs(u, h, w_t, w_c):
    bg, kc, cw = u.shape
    blk = lambda: pl.BlockSpec((1, kc, cw), lambda i: (i, 0, 0))
    wsp = lambda: pl.BlockSpec((1, cw, cw), lambda i: (i % SSM_GROUPS, 0, 0))
    return pl.pallas_call(
        _s5_y_kernel,
        out_shape=jax.ShapeDtypeStruct((bg, kc, cw), BF16),
        grid=(bg,),
        in_specs=[blk(), blk(), wsp(), wsp()],
        out_specs=blk(),
        compiler_params=_cp(("parallel",)),
        name="s5_outputs",
    )(u, h, w_t, w_c)


def _glu_kernel(y_ref, w_ref, o_ref):
    z = _dot(y_ref[...], w_ref[...])
    half = z.shape[1] // 2
    o_ref[...] = (z[:, :half] * jax.nn.sigmoid(z[:, half:])).astype(o_ref.dtype)


def glu_matmul(y, w, tm):
    t, kdim = y.shape
    return pl.pallas_call(
        _glu_kernel,
        out_shape=jax.ShapeDtypeStruct((t, w.shape[1] // 2), BF16),
        grid=(t // tm,),
        in_specs=[pl.BlockSpec((tm, kdim), lambda i: (i, 0)), _const_spec(w.shape)],
        out_specs=pl.BlockSpec((tm, w.shape[1] // 2), lambda i: (i, 0)),
        compiler_params=_cp(("parallel",)),
        name="glu_matmul",
    )(y, w)


def s5_mixer(u_tok, tables, w_glu, n_batch, seq, ctx_len):
    w_t, w_z, w_c, a_re, a_im = tables
    tc = SSM_CHUNK
    n = seq + ctx_len
    kc, k_lat, k_ctx = n // tc, seq // tc, ctx_len // tc
    u_seq = jnp.concatenate([u_tok[:n_batch * seq].reshape(n_batch, seq, MIX_W),
                             u_tok[n_batch * seq:].reshape(n_batch, ctx_len, MIX_W)], axis=1)
    u_ch = u_seq.reshape(n_batch, kc, tc, SSM_GROUPS, SSM_CH).transpose(0, 3, 1, 2, 4).reshape(n_batch * SSM_GROUPS, kc, tc * SSM_CH)
    z = s5_chunk_sums(u_ch, w_z)
    h = s5_state_scan(z.transpose(1, 0, 2), jnp.tile(a_re, (n_batch, 1)), jnp.tile(a_im, (n_batch, 1)), k_lat, k_ctx)
    y = s5_outputs(u_ch, h.transpose(1, 0, 2), w_t, w_c)
    y = y.reshape(n_batch, SSM_GROUPS, kc, tc, SSM_CH).transpose(0, 2, 3, 1, 4).reshape(n_batch, n, MIX_W)
    y_tok = jnp.concatenate([y[:, :seq].reshape(n_batch * seq, MIX_W), y[:, seq:].reshape(n_batch * ctx_len, MIX_W)], axis=0)
    return glu_matmul(y_tok, w_glu, 512)


def _chan_dft_kernel(x_ref, w_ref, yc_ref, ys_ref):
    x = x_ref[...]
    for g in range(FNO_GROUPS):
        y = _dot(x[:, FNO_CH * g:FNO_CH * (g + 1)], w_ref[...])
        yc_ref[:, FNO_CH * g:FNO_CH * (g + 1)] = y[:, :FNO_CH].astype(BF16)
        ys_ref[:, FNO_CH * g:FNO_CH * (g + 1)] = y[:, FNO_CH:].astype(BF16)


def chan_dft(x, w, tm):
    t, cw = x.shape
    return pl.pallas_call(
        _chan_dft_kernel,
        out_shape=[jax.ShapeDtypeStruct((t, cw), BF16)] * 2,
        grid=(t // tm,),
        in_specs=[pl.BlockSpec((tm, cw), lambda i: (i, 0)), _const_spec(w.shape)],
        out_specs=[pl.BlockSpec((tm, cw), lambda i: (i, 0))] * 2,
        compiler_params=_cp(("parallel",)),
        name="chan_dft",
    )(x, w)


def _pos_dft_kernel(c_ref, s_ref, yc_ref, ys_ref, o_ref, acc_ref, *, n_k):
    j = pl.program_id(2)

    @pl.when(j == 0)
    def _():
        acc_ref[...] = jnp.zeros_like(acc_ref)

    acc_ref[...] += _dot(c_ref[...], yc_ref[0]) - _dot(s_ref[...], ys_ref[0])

    @pl.when(j == n_k - 1)
    def _():
        o_ref[0] = acc_ref[...].astype(o_ref.dtype)


def pos_dft(cos_t, sin_t, yc, ys, tm, tk):
    nb, n, w = yc.shape
    n_k = n // tk
    return pl.pallas_call(
        functools.partial(_pos_dft_kernel, n_k=n_k),
        out_shape=jax.ShapeDtypeStruct((nb, n, w), BF16),
        grid=(nb, n // tm, n_k),
        in_specs=[pl.BlockSpec((tm, tk), lambda b, i, j: (i, j)), pl.BlockSpec((tm, tk), lambda b, i, j: (i, j)),
                  pl.BlockSpec((1, tk, w), lambda b, i, j: (b, j, 0)), pl.BlockSpec((1, tk, w), lambda b, i, j: (b, j, 0))],
        out_specs=pl.BlockSpec((1, tm, w), lambda b, i, j: (b, i, 0)),
        scratch_shapes=[pltpu.VMEM((tm, w), F32)],
        compiler_params=_cp(("parallel", "parallel", "arbitrary")),
        name="pos_dft",
    )(cos_t, sin_t, yc, ys)


def dft_tables(n):
    idx = (jnp.arange(n, dtype=jnp.int32)[:, None] * jnp.arange(n, dtype=jnp.int32)[None, :]) % n
    ang = idx.astype(F32) * (2.0 * math.pi / n)
    return jnp.cos(ang).astype(BF16), jnp.sin(ang).astype(BF16)


def chan_table(n_pos):
    c = jnp.arange(FNO_CH, dtype=jnp.int32)
    ang = ((c[:, None] * c[None, :]) % FNO_CH).astype(F32) * (2.0 * math.pi / FNO_CH)
    scale = 1.0 / math.sqrt(n_pos * FNO_CH)
    return (jnp.concatenate([jnp.cos(ang), jnp.sin(ang)], axis=1) * scale).astype(BF16)


def fourier_mixer(x_seq, pos_tabs, tm, tk):
    nb, n, w = x_seq.shape
    yc, ys = chan_dft(x_seq.reshape(nb * n, w), chan_table(n), min(512, n))
    return pos_dft(pos_tabs[0], pos_tabs[1], yc.reshape(nb, n, w), ys.reshape(nb, n, w), tm, tk)


def _route(logits_t, bias_col):
    score = jax.nn.sigmoid(logits_t)
    sel = score + bias_col
    sa, sb, sc, sd = (sel[8 * i:8 * (i + 1)] for i in range(EXPERTS_PER_GROUP))
    ra, rb, rc, rd = (score[8 * i:8 * (i + 1)] for i in range(EXPERTS_PER_GROUP))
    m1, n1 = jnp.maximum(sa, sb), jnp.minimum(sa, sb)
    m2, n2 = jnp.maximum(sc, sd), jnp.minimum(sc, sd)
    gsum = jnp.maximum(m1, m2) + jnp.maximum(jnp.minimum(m1, m2), jnp.maximum(n1, n2))
    rows = lax.broadcasted_iota(jnp.int32, gsum.shape, 0)
    neg = jnp.float32(-jnp.inf)
    gsum = jnp.where(rows < N_EXPERT_GROUPS, gsum, neg)
    best = jnp.max(gsum, axis=0, keepdims=True)
    grp_f = jnp.min(jnp.where(gsum == best, rows.astype(F32), 8.0), axis=0, keepdims=True)
    grp = grp_f.astype(jnp.int32)
    hot = rows == grp
    pick = lambda v: jnp.sum(jnp.where(hot, v, 0.0), axis=0, keepdims=True)
    a, b, c, d = pick(sa), pick(sb), pick(sc), pick(sd)
    wa, wb, wc, wd = pick(ra), pick(rb), pick(rc), pick(rd)

    def first_max(va, vb, vc, vd):
        m = jnp.maximum(jnp.maximum(va, vb), jnp.maximum(vc, vd))
        return jnp.where(va == m, 0, jnp.where(vb == m, 1, jnp.where(vc == m, 2, 3)))

    i1 = first_max(a, b, c, d)
    i2 = first_max(jnp.where(i1 == 0, neg, a), jnp.where(i1 == 1, neg, b), jnp.where(i1 == 2, neg, c), jnp.where(i1 == 3, neg, d))
    gate = lambda i: jnp.where(i == 0, wa, jnp.where(i == 1, wb, jnp.where(i == 2, wc, wd)))
    g1, g2 = gate(i1), gate(i2)
    den = g1 + g2
    return grp * EXPERTS_PER_GROUP + i1, grp * EXPERTS_PER_GROUP + i2, g1 / den, g2 / den


def _outproj_kernel(pa_ref, pb_ref, pc_ref, pd_ref, x_ref, w_ref, g1_ref, gam_ref, sh_ref, sc_ref, rw_ref, rb_ref,
                    xo_ref, h_ref, e_ref, gt_ref):
    acc = _dot(pa_ref[...], w_ref[0:MIX_W])
    for i, p_ref in enumerate((pb_ref, pc_ref, pd_ref), start=1):
        acc += _dot(p_ref[...], w_ref[MIX_W * i:MIX_W * (i + 1)])
    x = x_ref[...] + g1_ref[0] * acc
    xo_ref[...] = x
    h = _rms(x, gam_ref[...]) * (1.0 + sc_ref[0]) + sh_ref[0]
    h_ref[...] = h.astype(BF16)
    logits = _dot3(h, rw_ref[...])
    e1, e2, g1, g2 = _route(logits.T[:32], rb_ref[...])
    rows = lax.broadcasted_iota(jnp.int32, e_ref.shape, 0)
    e_ref[...] = jnp.where(rows == 0, e1, jnp.where(rows == 1, e2, 0))
    gt_ref[...] = jnp.where(rows == 0, g1, jnp.where(rows == 1, g2, 0.0))


def out_projection(parts, tok, w_out, gate1, gamma, shift, scale, rw, rb, n_rows, lat_tiles_per_batch, n_batch, tm):
    d = tok.shape[1]
    mod_idx = lambda i: (jnp.minimum(i // lat_tiles_per_batch, n_batch), 0, 0)
    row = lambda w: pl.BlockSpec((tm, w), lambda i: (i, 0))
    mod = lambda: pl.BlockSpec((1, 1, d), mod_idx)
    return pl.pallas_call(
        _outproj_kernel,
        out_shape=[jax.ShapeDtypeStruct((n_rows, d), F32), jax.ShapeDtypeStruct((n_rows, d), BF16),
                   jax.ShapeDtypeStruct((8, n_rows), jnp.int32), jax.ShapeDtypeStruct((8, n_rows), F32)],
        grid=(n_rows // tm,),
        in_specs=[row(MIX_W), row(MIX_W), row(MIX_W), row(MIX_W), row(d),
                  pl.BlockSpec(w_out.shape, lambda i: (0, 0), pipeline_mode=pl.Buffered(1)),
                  mod(), _const_spec((1, d)), mod(), mod(), _const_spec(rw.shape), _const_spec(rb.shape)],
        out_specs=[row(d), row(d), pl.BlockSpec((8, tm), lambda i: (0, i)), pl.BlockSpec((8, tm), lambda i: (0, i))],
        compiler_params=_cp(("parallel",)),
        name="out_projection",
    )(*parts, tok, w_out, gate1, gamma.reshape(1, d), shift, scale, rw, rb)


def _moe_kernel(be_ref, nu_ref, x_ref, g_ref, wg_ref, wu_ref, wd_ref, y_ref):
    i = pl.program_id(0)

    @pl.when(i < nu_ref[0])
    def _():
        x = x_ref[...]
        a = (jax.nn.silu(_dot(x, wg_ref[0])) * _dot(x, wu_ref[0])).astype(BF16)
        y_ref[...] = _dot(a, wd_ref[0]) * g_ref[...]

    @pl.when(i >= nu_ref[0])
    def _():
        y_ref[...] = jnp.zeros_like(y_ref)


def moe_experts(xb, row_gate, block_expert, n_used, wg, wu, wd):
    n_rows, d = xb.shape
    n_blocks = n_rows // MOE_BLOCK
    de = wg.shape[2]
    grid_spec = pltpu.PrefetchScalarGridSpec(
        num_scalar_prefetch=2,
        grid=(n_blocks,),
        in_specs=[pl.BlockSpec((MOE_BLOCK, d), lambda i, be, nu: (i, 0)), pl.BlockSpec((MOE_BLOCK, 1), lambda i, be, nu: (i, 0)),
                  pl.BlockSpec((1, d, de), lambda i, be, nu: (be[i], 0, 0)), pl.BlockSpec((1, d, de), lambda i, be, nu: (be[i], 0, 0)),
                  pl.BlockSpec((1, de, d), lambda i, be, nu: (be[i], 0, 0))],
        out_specs=pl.BlockSpec((MOE_BLOCK, d), lambda i, be, nu: (i, 0)),
    )
    return pl.pallas_call(
        _moe_kernel,
        out_shape=jax.ShapeDtypeStruct((n_rows, d), F32),
        grid_spec=grid_spec,
        compiler_params=_cp(("arbitrary",)),
        name="moe_experts",
    )(block_expert, n_used, xb, row_gate, wg, wu, wd)


def routed_moe(h, eid, gate, wg, wu, wd):
    t = h.shape[0]
    bm = MOE_BLOCK
    flat_e = eid[:2].reshape(-1)
    flat_g = gate[:2].reshape(-1)
    flat_tok = jnp.tile(jnp.arange(t, dtype=jnp.int32), 2)
    hot = (flat_e[:, None] == jnp.arange(N_EXPERTS, dtype=jnp.int32)[None, :]).astype(jnp.int32)
    csum = jnp.cumsum(hot, axis=0)
    counts = csum[-1]
    padded = (counts + bm - 1) // bm * bm
    pad_end = jnp.cumsum(padded)
    pad_start = pad_end - padded
    dest = jnp.sum(hot * (pad_start[None, :] + csum - 1), axis=1)
    n_blocks = (2 * t + N_EXPERTS * (bm - 1) + bm - 1) // bm
    n_rows = n_blocks * bm
    row_tok = jnp.zeros((n_rows,), jnp.int32).at[dest].set(flat_tok)
    row_gate = jnp.zeros((n_rows,), F32).at[dest].set(flat_g)
    block_expert = jnp.minimum(jnp.searchsorted(pad_end, jnp.arange(n_blocks, dtype=jnp.int32) * bm, side='right'),
                               N_EXPERTS - 1).astype(jnp.int32)
    n_used = (pad_end[-1] // bm).astype(jnp.int32).reshape(1)
    yb = moe_experts(h[row_tok], row_gate.reshape(n_rows, 1), block_expert, n_used, wg, wu, wd)
    pos = dest.reshape(2, t)
    return yb[pos[0]] + yb[pos[1]]


def _residual_kernel(x_ref, y_ref, g_ref, o_ref):
    o_ref[...] = x_ref[...] + g_ref[0] * y_ref[...]


def _final_kernel(x_ref, y_ref, g_ref, gam_ref, o_ref):
    o_ref[...] = _rms(x_ref[...] + g_ref[0] * y_ref[...], gam_ref[...])


def gated_residual(x, y, gate, lat_tiles_per_batch, n_batch, tm, final_gamma=None):
    t, d = x.shape
    mod_idx = lambda i: (jnp.minimum(i // lat_tiles_per_batch, n_batch), 0, 0)
    row = lambda: pl.BlockSpec((tm, d), lambda i: (i, 0))
    in_specs = [row(), row(), pl.BlockSpec((1, 1, d), mod_idx)]
    args = [x, y, gate]
    body = _residual_kernel
    if final_gamma is not None:
        in_specs.append(_const_spec((1, d)))
        args.append(final_gamma.reshape(1, d))
        body = _final_kernel
    return pl.pallas_call(
        body,
        out_shape=jax.ShapeDtypeStruct((t, d), F32),
        grid=(t // tm,),
        in_specs=in_specs,
        out_specs=row(),
        compiler_params=_cp(("parallel",)),
        name="gated_residual",
    )(*args)


def _rope_tables(seq, ctx_len):
    pos = jnp.arange(seq)
    row = (pos // GRID_W).astype(F32)
    col = (pos % GRID_W).astype(F32)

    def tab(dim):
        quarter = dim // 4
        inv = ROPE_THETA ** (-jnp.arange(quarter, dtype=F32) / quarter)
        ang = jnp.concatenate([row[:, None] * inv, col[:, None] * inv], axis=-1)
        cos, sin = jnp.cos(ang), jnp.sin(ang)
        cos_f = jnp.concatenate([cos, cos], axis=-1)
        sin_f = jnp.concatenate([-sin, sin], axis=-1)
        cos_f = jnp.concatenate([cos_f, jnp.ones((ctx_len, dim), F32)], axis=0)
        sin_f = jnp.concatenate([sin_f, jnp.zeros((ctx_len, dim), F32)], axis=0)
        pad = LANE - dim
        return jnp.pad(cos_f, ((0, 0), (0, pad))), jnp.pad(sin_f, ((0, 0), (0, pad)))

    cm, sm = tab(MLA_ROPE)
    cg, sg = tab(GQA_HD)
    return cm, sm, cg, sg


def _half_swap(w, dim):
    return jnp.concatenate([w[..., dim // 2:], w[..., :dim // 2]], axis=-1)


def _layout_w_in(w):
    d = w.shape[0]
    k_rot = w[:, MLA_Q_RANK + MLA_KV_RANK:MLA_IN]
    z = jnp.zeros((d, LANE - MLA_ROPE), w.dtype)
    return jnp.concatenate([w[:, :MLA_Q_RANK + MLA_KV_RANK], k_rot, z, _half_swap(k_rot, MLA_ROPE), z, w[:, OFF_B:]], axis=1).astype(BF16)


def _layout_w_uq(w):
    r = w.shape[0]
    wh = w.reshape(r, MLA_HEADS, MLA_NOPE + MLA_ROPE)
    rope = wh[..., MLA_NOPE:]
    z = jnp.zeros((r, MLA_HEADS, LANE - MLA_ROPE), w.dtype)
    main = jnp.concatenate([wh[..., :MLA_NOPE], rope, z], axis=-1).reshape(r, MLA_HEADS * MLA_DQ)
    rot = jnp.concatenate([_half_swap(rope, MLA_ROPE), z], axis=-1).reshape(r, MLA_HEADS * LANE)
    return jnp.concatenate([main, rot], axis=1).astype(BF16)


def _layout_router(router_w, router_bias):
    d = router_w.shape[0]
    w4 = router_w.reshape(d, N_EXPERT_GROUPS, EXPERTS_PER_GROUP).transpose(0, 2, 1)
    w = jnp.pad(w4, ((0, 0), (0, 0), (0, 8 - N_EXPERT_GROUPS))).reshape(d, 8 * EXPERTS_PER_GROUP)
    w = jnp.pad(w, ((0, 0), (0, LANE - 8 * EXPERTS_PER_GROUP))).astype(F32)
    b4 = router_bias.astype(F32).reshape(N_EXPERT_GROUPS, EXPERTS_PER_GROUP).T
    b = jnp.pad(b4, ((0, 0), (0, 8 - N_EXPERT_GROUPS))).reshape(8 * EXPERTS_PER_GROUP, 1)
    return w, b


def kernel(x, c, ctx, c_ctx, ada_w, ada_b, norm1_g, norm2_g, w_in, mla_q_norm_g, mla_w_uq, mla_kv_norm_g, mla_w_ukv, gqa_q_norm_g, gqa_k_norm_g, ssm_lam_re, ssm_lam_im, ssm_log_dt, ssm_b_re, ssm_b_im, ssm_c_re, ssm_c_im, ssm_d, ssm_w_glu, w_out, router_w, router_bias, moe_w_gate, moe_w_up, moe_w_down, final_norm_g):
    nb, seq, d = x.shape
    ctx_len = ctx.shape[1]
    depth = ada_w.shape[0]
    n_lat, n_ctx = nb * seq, nb * ctx_len
    tm = min(512, n_ctx)
    lat_tiles = seq // tm
    rows_q = min(1024, seq)
    tk_lat = min(2048, seq)
    in_widths = (768, GQA_IN, MIX_W, MIX_W)

    tok = jnp.concatenate([x.reshape(n_lat, d), ctx.reshape(n_ctx, d)], axis=0)
    silu = jnp.concatenate([jax.nn.silu(c), jnp.broadcast_to(jax.nn.silu(c_ctx)[None], (8 - nb, d))], axis=0)
    rope_tabs = _rope_tables(seq, ctx_len)
    rw, rb = _layout_router(router_w, router_bias)
    dft_lat = dft_tables(seq)
    dft_ctx = dft_tables(ctx_len)

    for l in range(depth):
        need_ctx = l < depth - 1
        mod = mod_vectors(silu, ada_w[l], ada_b[l])[:nb + 1].reshape(nb + 1, 1, 6, d)
        sh1, sc1, g1, sh2, sc2, g2 = (mod[:, :, i] for i in range(6))

        pa, pb, pc, pd = in_projection(tok, norm1_g[l], sh1, sc1, _layout_w_in(w_in[l]), in_widths, lat_tiles, nb, tm)

        qm, km, vm, qg, kg, vg = qkv_prepare(pa, pb, rope_tabs, mla_q_norm_g[l], mla_kv_norm_g[l], gqa_q_norm_g[l],
                                             gqa_k_norm_g[l], _layout_w_uq(mla_w_uq[l]), mla_w_ukv[l].astype(BF16),
                                             nb, seq, ctx_len, min(256, ctx_len))
        qm5 = qm.reshape(nb, MLA_HEADS, 1, seq + ctx_len, MLA_DQ)
        out_a = flash_attention(qm5, km, vm, seq, ctx_len, rows_q, tk_lat).reshape(n_lat, MIX_W)
        out_b = flash_attention(qg, kg, vg, seq, ctx_len, rows_q // qg.shape[2], tk_lat).reshape(n_lat, MIX_W)

        s5_tabs = s5_tables(ssm_lam_re[l], ssm_lam_im[l], ssm_log_dt[l], ssm_b_re[l], ssm_b_im[l], ssm_c_re[l], ssm_c_im[l], ssm_d[l])
        out_c = s5_mixer(pc, s5_tabs, ssm_w_glu[l].astype(BF16), nb, seq, ctx_len)

        out_d = fourier_mixer(pd[:n_lat].reshape(nb, seq, MIX_W), dft_lat, min(1024, seq), min(2048, seq)).reshape(n_lat, MIX_W)

        if need_ctx:
            ctx_a = context_attention(qm5, km, vm, seq, ctx_len).reshape(n_ctx, MIX_W)
            ctx_b = context_attention(qg, kg, vg, seq, ctx_len).reshape(n_ctx, MIX_W)
            ctx_d = fourier_mixer(pd[n_lat:].reshape(nb, ctx_len, MIX_W), dft_ctx, ctx_len, ctx_len).reshape(n_ctx, MIX_W)
            parts = [jnp.concatenate([out_a, ctx_a], axis=0), jnp.concatenate([out_b, ctx_b], axis=0), out_c,
                     jnp.concatenate([out_d, ctx_d], axis=0)]
            n_rows = n_lat + n_ctx
        else:
            parts = [out_a, out_b, out_c, out_d]
            n_rows = n_lat

        tok, h2, eid, gate = out_projection(parts, tok, w_out[l].astype(BF16), g1, norm2_g[l], sh2, sc2, rw, rb,
                                            n_rows, 2 * lat_tiles, nb, tm // 2)
        y = routed_moe(h2, eid, gate, moe_w_gate[l].astype(BF16), moe_w_up[l].astype(BF16), moe_w_down[l].astype(BF16))
        tok = gated_residual(tok, y, g2, lat_tiles, nb, tm, None if need_ctx else final_norm_g)

    return tok[:n_lat].reshape(nb, seq, d)
```

```python
import functools
import math

import jax
import jax.numpy as jnp
from jax import lax
from jax.experimental import pallas as pl
from jax.experimental.pallas import tpu as pltpu

F32 = jnp.float32
BF16 = jnp.bfloat16

D_MODEL = 2048
GRID_W = 64
ROPE_THETA = 10000.0
RMS_EPS = 1e-6
MIX_W = D_MODEL // 4
MLA_HEADS = 4
MLA_NOPE = 128
MLA_V = 128
MLA_ROPE = 64
MLA_Q_RANK = 384
MLA_KV_RANK = 128
MLA_IN = MLA_Q_RANK + MLA_KV_RANK + MLA_ROPE
MLA_DQ = 256
GQA_HEADS = 4
GQA_KV_HEADS = 2
GQA_HD = 128
GQA_IN = (GQA_HEADS + 2 * GQA_KV_HEADS) * GQA_HD
SSM_GROUPS = 32
SSM_CH = 16
SSM_STATE = 64
SSM_CHUNK = 16
FNO_GROUPS = 4
FNO_CH = 128
OFF_B = MLA_IN
OFF_C = OFF_B + GQA_IN
OFF_D = OFF_C + MIX_W
N_EXPERTS = 16
N_EXPERT_GROUPS = 4
EXPERTS_PER_GROUP = 4
D_EXPERT = D_MODEL // 2
MOE_BLOCK = 256
LANE = 128
MXU_W = 256
LOG2E = 1.4426950408889634
VMEM_LIMIT = 52 * 1024 * 1024


def _cp(sem, vmem=VMEM_LIMIT):
    return pltpu.CompilerParams(dimension_semantics=sem, vmem_limit_bytes=vmem)


def _const_spec(shape):
    nd = len(shape)
    return pl.BlockSpec(shape, lambda *_: (0,) * nd)


def _split_bf16(a):
    hi = a.astype(BF16)
    lo = (a - hi.astype(F32)).astype(BF16)
    return hi, lo


def _dot(a, b):
    return jnp.dot(a, b, preferred_element_type=F32)


def _dot3(a, b):
    ah, al = _split_bf16(a)
    bh, bl = _split_bf16(b)
    return _dot(ah, bh) + _dot(al, bh) + _dot(ah, bl)


def _rms(x, g):
    return x * lax.rsqrt(jnp.mean(x * x, axis=-1, keepdims=True) + RMS_EPS) * g


def _mod_kernel(s_ref, w_ref, b_ref, o_ref):
    o_ref[...] = _dot3(s_ref[...], w_ref[...]) + b_ref[...]


def mod_vectors(s8, w, b):
    k, n = w.shape
    tn = 1024
    return pl.pallas_call(
        _mod_kernel,
        out_shape=jax.ShapeDtypeStruct((8, n), F32),
        grid=(n // tn,),
        in_specs=[_const_spec((8, k)), pl.BlockSpec((k, tn), lambda j: (0, j)), pl.BlockSpec((1, tn), lambda j: (0, j))],
        out_specs=pl.BlockSpec((8, tn), lambda j: (0, j)),
        compiler_params=_cp(("parallel",)),
        name="mod_vectors",
    )(s8, w, b.reshape(1, n))


def _inproj_kernel(x_ref, g_ref, sh_ref, sc_ref, w_ref, *o_refs, widths):
    h = _rms(x_ref[...], g_ref[...])
    hb = (h * (1.0 + sc_ref[0]) + sh_ref[0]).astype(BF16)
    off = 0
    for o_ref, wd in zip(o_refs, widths):
        o_ref[...] = _dot(hb, w_ref[:, off:off + wd]).astype(o_ref.dtype)
        off += wd


def in_projection(tok, gamma, shift, scale, w, widths, lat_tiles_per_batch, n_batch, tm):
    t, d = tok.shape
    mod_idx = lambda i: (jnp.minimum(i // lat_tiles_per_batch, n_batch), 0, 0)
    return pl.pallas_call(
        functools.partial(_inproj_kernel, widths=widths),
        out_shape=[jax.ShapeDtypeStruct((t, wd), BF16) for wd in widths],
        grid=(t // tm,),
        in_specs=[pl.BlockSpec((tm, d), lambda i: (i, 0)), _const_spec((1, d)),
                  pl.BlockSpec((1, 1, d), mod_idx), pl.BlockSpec((1, 1, d), mod_idx),
                  pl.BlockSpec(w.shape, lambda i: (0, 0), pipeline_mode=pl.Buffered(1))],
        out_specs=[pl.BlockSpec((tm, wd), lambda i: (i, 0)) for wd in widths],
        compiler_params=_cp(("parallel",)),
        name="in_projection",
    )(tok, gamma.reshape(1, d), shift, scale, w)


def _qkv_kernel(a_ref, b_ref, cm_ref, sm_ref, cg_ref, sg_ref, gq_ref, gkv_ref, ggq_ref, ggk_ref, wuq_ref, wukv_ref,
                qm_ref, km_ref, vm_ref, qg_ref, kg_ref, vg_ref):
    a = a_ref[...].astype(F32)
    cm, sm = cm_ref[...], sm_ref[...]
    mla_scale = LOG2E * (MLA_NOPE + MLA_ROPE) ** -0.5
    qn = _rms(a[:, :MLA_Q_RANK], gq_ref[...]).astype(BF16)
    q_all = _dot(qn, wuq_ref[...])
    rot0 = MLA_HEADS * MLA_DQ
    for h in range(MLA_HEADS):
        nope = q_all[:, MLA_DQ * h:MLA_DQ * h + LANE]
        rp = q_all[:, MLA_DQ * h + LANE:MLA_DQ * (h + 1)]
        rr = q_all[:, rot0 + LANE * h:rot0 + LANE * (h + 1)]
        qm_ref[0, h, :, :LANE] = (nope * mla_scale).astype(BF16)
        qm_ref[0, h, :, LANE:] = ((rp * cm + rr * sm) * mla_scale).astype(BF16)
    kvn = _rms(a[:, MLA_Q_RANK:MLA_Q_RANK + MLA_KV_RANK], gkv_ref[...]).astype(BF16)
    kv = _dot(kvn, wukv_ref[...])
    kr = (a[:, 512:640] * cm + a[:, 640:768] * sm).astype(BF16)
    for h in range(MLA_HEADS):
        km_ref[0, h, :, :LANE] = kv[:, 256 * h:256 * h + LANE].astype(BF16)
        km_ref[0, h, :, LANE:] = kr
        vm_ref[0, h] = kv[:, 256 * h + LANE:256 * (h + 1)].T.astype(BF16)
    b = b_ref[...].astype(F32)
    cg, sg = cg_ref[...], sg_ref[...]
    gqa_scale = LOG2E * GQA_HD ** -0.5
    grp = GQA_HEADS // GQA_KV_HEADS
    for h in range(GQA_HEADS):
        qh = _rms(b[:, GQA_HD * h:GQA_HD * (h + 1)], ggq_ref[...])
        qh = qh * cg + pltpu.roll(qh, GQA_HD // 2, 1) * sg
        qg_ref[0, h // grp, h % grp] = (qh * gqa_scale).astype(BF16)
    k0 = GQA_HEADS * GQA_HD
    v0 = k0 + GQA_KV_HEADS * GQA_HD
    for h in range(GQA_KV_HEADS):
        kh = _rms(b[:, k0 + GQA_HD * h:k0 + GQA_HD * (h + 1)], ggk_ref[...])
        kg_ref[0, h] = (kh * cg + pltpu.roll(kh, GQA_HD // 2, 1) * sg).astype(BF16)
        vg_ref[0, h] = b[:, v0 + GQA_HD * h:v0 + GQA_HD * (h + 1)].T.astype(BF16)

def qkv_prepare(pa, pb, tabs, g_q, g_kv, g_gq, g_gk, w_uq, w_ukv, n_batch, seq, ctx_len, tm):
    t = pa.shape[0]
    n = seq + ctx_len
    lat_tiles = seq // tm
    ctx_tiles = ctx_len // tm
    n_lat = n_batch * lat_tiles

    def bidx(i):
        j = i - n_lat
        bi = jnp.where(i < n_lat, i // lat_tiles, j // ctx_tiles)
        blk = jnp.where(i < n_lat, i % lat_tiles, lat_tiles + j % ctx_tiles)
        return bi, blk

    def o4(i):
        bi, blk = bidx(i)
        return (bi, 0, blk, 0)

    def o4t(i):
        bi, blk = bidx(i)
        return (bi, 0, 0, blk)

    def o5(i):
        bi, blk = bidx(i)
        return (bi, 0, 0, blk, 0)

    tab_spec = pl.BlockSpec((tm, LANE), lambda i: (bidx(i)[1], 0))
    vec = lambda w: _const_spec((1, w))
    return pl.pallas_call(
        _qkv_kernel,
        out_shape=[jax.ShapeDtypeStruct((n_batch, MLA_HEADS, n, MLA_DQ), BF16),
                   jax.ShapeDtypeStruct((n_batch, MLA_HEADS, n, MLA_DQ), BF16),
                   jax.ShapeDtypeStruct((n_batch, MLA_HEADS, MLA_V, n), BF16),
                   jax.ShapeDtypeStruct((n_batch, GQA_KV_HEADS, GQA_HEADS // GQA_KV_HEADS, n, GQA_HD), BF16),
                   jax.ShapeDtypeStruct((n_batch, GQA_KV_HEADS, n, GQA_HD), BF16),
                   jax.ShapeDtypeStruct((n_batch, GQA_KV_HEADS, GQA_HD, n), BF16)],
        grid=(t // tm,),
        in_specs=[pl.BlockSpec((tm, pa.shape[1]), lambda i: (i, 0)), pl.BlockSpec((tm, pb.shape[1]), lambda i: (i, 0)),
                  tab_spec, tab_spec, tab_spec, tab_spec,
                  vec(MLA_Q_RANK), vec(MLA_KV_RANK), vec(GQA_HD), vec(GQA_HD),
                  _const_spec(w_uq.shape), _const_spec(w_ukv.shape)],
        out_specs=[pl.BlockSpec((1, MLA_HEADS, tm, MLA_DQ), o4), pl.BlockSpec((1, MLA_HEADS, tm, MLA_DQ), o4),
                   pl.BlockSpec((1, MLA_HEADS, MLA_V, tm), o4t),
                   pl.BlockSpec((1, GQA_KV_HEADS, GQA_HEADS // GQA_KV_HEADS, tm, GQA_HD), o5),
                   pl.BlockSpec((1, GQA_KV_HEADS, tm, GQA_HD), o4), pl.BlockSpec((1, GQA_KV_HEADS, GQA_HD, tm), o4t)],
        compiler_params=_cp(("parallel",)),
        name="qkv_prepare",
    )(pa, pb, *tabs, g_q.reshape(1, -1), g_kv.reshape(1, -1), g_gq.reshape(1, -1), g_gk.reshape(1, -1), w_uq, w_ukv)


def _dot_nt(a, b):
    return lax.dot_general(a, b, (((1,), (1,)), ((), ())), preferred_element_type=F32)


def _softmax_cols(s):
    m = jnp.max(s, axis=0, keepdims=True)
    p = jnp.exp2(s - m)
    return m, p, jnp.sum(p, axis=0, keepdims=True)


def _store_heads(o_ref, o_t, grp, tq, dv):
    o = o_t.T
    for g in range(grp):
        o_ref[0, :, dv * g:dv * (g + 1)] = o[tq * g:tq * (g + 1)].astype(o_ref.dtype)


def _flash_kernel(q_ref, k_ref, vt_ref, kc_ref, vtc_ref, o_ref, m_sc, l_sc, acc_sc, *, grp, tq, dv, n_kv):
    j = pl.program_id(3)
    rows = grp * tq
    q = q_ref[0, 0].reshape(rows, q_ref.shape[-1])

    @pl.when(j == 0)
    def _():
        m, p, l = _softmax_cols(_dot_nt(kc_ref[0, 0], q))
        m_sc[...] = m
        l_sc[...] = l
        acc_sc[...] = _dot(vtc_ref[0, 0], p.astype(BF16))

    k = k_ref[0, 0]
    vt = vt_ref[0, 0]
    slabs = [slice(MXU_W * r, MXU_W * (r + 1)) for r in range(rows // MXU_W)]
    scores = [_dot_nt(k, q[sl]) for sl in slabs]
    for sl, s in zip(slabs, scores):
        m_prev = m_sc[:, sl]
        m_new = jnp.maximum(m_prev, jnp.max(s, axis=0, keepdims=True))
        alpha = jnp.exp2(m_prev - m_new)
        p = jnp.exp2(s - m_new)
        l_sc[:, sl] = alpha * l_sc[:, sl] + jnp.sum(p, axis=0, keepdims=True)
        acc_sc[:, sl] = alpha * acc_sc[:, sl] + _dot(vt, p.astype(BF16))
        m_sc[:, sl] = m_new

    @pl.when(j == n_kv - 1)
    def _():
        _store_heads(o_ref, acc_sc[...] / l_sc[...], grp, tq, dv)


def flash_attention(q, k, vt, seq, ctx_len, tq, tk):
    nb, hk, grp, n, dq = q.shape
    dv = vt.shape[-2]
    n_kv = seq // tk
    cb = seq // ctx_len
    rows = grp * tq
    return pl.pallas_call(
        functools.partial(_flash_kernel, grp=grp, tq=tq, dv=dv, n_kv=n_kv),
        out_shape=jax.ShapeDtypeStruct((nb, seq, hk * grp * dv), BF16),
        grid=(nb, hk, seq // tq, n_kv),
        in_specs=[pl.BlockSpec((1, 1, grp, tq, dq), lambda b, h, i, j: (b, h, 0, i, 0)),
                  pl.BlockSpec((1, 1, tk, dq), lambda b, h, i, j: (b, h, j, 0)),
                  pl.BlockSpec((1, 1, dv, tk), lambda b, h, i, j: (b, h, 0, j)),
                  pl.BlockSpec((1, 1, ctx_len, dq), lambda b, h, i, j: (b, h, cb, 0)),
                  pl.BlockSpec((1, 1, dv, ctx_len), lambda b, h, i, j: (b, h, 0, cb))],
        out_specs=pl.BlockSpec((1, tq, grp * dv), lambda b, h, i, j: (b, i, h)),
        scratch_shapes=[pltpu.VMEM((1, rows), F32), pltpu.VMEM((1, rows), F32), pltpu.VMEM((dv, rows), F32)],
        compiler_params=_cp(("parallel", "parallel", "parallel", "arbitrary")),
        name="flash_attention",
    )(q, k, vt, k, vt)


def _ctx_attn_kernel(q_ref, kc_ref, vtc_ref, o_ref, *, grp, tq, dv):
    q = q_ref[0, 0].reshape(grp * tq, q_ref.shape[-1])
    _, p, l = _softmax_cols(_dot_nt(kc_ref[0, 0], q))
    _store_heads(o_ref, _dot(vtc_ref[0, 0], p.astype(BF16)) / l, grp, tq, dv)


def context_attention(q, k, vt, seq, ctx_len):
    nb, hk, grp, n, dq = q.shape
    dv = vt.shape[-2]
    cb = seq // ctx_len
    return pl.pallas_call(
        functools.partial(_ctx_attn_kernel, grp=grp, tq=ctx_len, dv=dv),
        out_shape=jax.ShapeDtypeStruct((nb, ctx_len, hk * grp * dv), BF16),
        grid=(nb, hk),
        in_specs=[pl.BlockSpec((1, 1, grp, ctx_len, dq), lambda b, h: (b, h, 0, cb, 0)),
                  pl.BlockSpec((1, 1, ctx_len, dq), lambda b, h: (b, h, cb, 0)),
                  pl.BlockSpec((1, 1, dv, ctx_len), lambda b, h: (b, h, 0, cb))],
        out_specs=pl.BlockSpec((1, ctx_len, grp * dv), lambda b, h: (b, 0, h)),
        compiler_params=_cp(("parallel", "parallel")),
        name="context_attention",
    )(q, k, vt)


def s5_tables(lam_re, lam_im, log_dt, b_re, b_im, c_re, c_im, d):
    hp = lax.Precision.HIGHEST
    tc = SSM_CHUNK
    lam = lax.complex(lam_re.astype(F32), lam_im.astype(F32))
    dt = jnp.exp(log_dt.astype(F32))[..., None]
    lam_bar = jnp.exp(lam * dt)
    b_bar = ((lam_bar - 1.0) / lam)[..., None] * lax.complex(b_re.astype(F32), b_im.astype(F32))
    c_mat = lax.complex(c_re.astype(F32), c_im.astype(F32))
    steps = jnp.arange(tc + 1, dtype=F32)
    lam_pow = jnp.exp((lam * dt)[:, :, None, :] * steps[None, None, :, None])
    kern = jnp.einsum('dgip,dgtp,dgpj->dgtij', c_mat, lam_pow[:, :, :tc], b_bar, precision=hp).real
    s_idx = jnp.arange(tc)[:, None]
    t_idx = jnp.arange(tc)[None, :]
    kf = kern[0][:, jnp.clip(t_idx - s_idx, 0, tc - 1)] * (t_idx >= s_idx)[None, :, :, None, None]
    kb = kern[1][:, jnp.clip(s_idx - t_idx, 0, tc - 1)] * (s_idx >= t_idx)[None, :, :, None, None]
    dg = d.astype(F32).reshape(SSM_GROUPS, SSM_CH)
    skip = (s_idx == t_idx)[None, :, :, None, None] * (jnp.eye(SSM_CH, dtype=F32) * dg[:, :, None])[:, None, None]
    w_t = (kf + kb + skip).transpose(0, 1, 4, 2, 3).reshape(SSM_GROUPS, tc * SSM_CH, tc * SSM_CH)
    zf = lam_pow[0][:, tc - 1 - jnp.arange(tc)][..., None] * b_bar[0][:, None]
    zb = lam_pow[1][:, jnp.arange(tc)][..., None] * b_bar[1][:, None]
    to_rows = lambda z: z.transpose(0, 1, 3, 2).reshape(SSM_GROUPS, tc * SSM_CH, SSM_STATE)
    w_z = jnp.concatenate([to_rows(zf.real), to_rows(zb.real), to_rows(zf.imag), to_rows(zb.imag)], axis=-1)
    mf = c_mat[0][:, None] * lam_pow[0][:, 1 + jnp.arange(tc)][:, :, None, :]
    mb = c_mat[1][:, None] * lam_pow[1][:, tc - jnp.arange(tc)][:, :, None, :]
    to_cols = lambda m: m.transpose(0, 3, 1, 2).reshape(SSM_GROUPS, SSM_STATE, tc * SSM_CH)
    w_c = jnp.concatenate([to_cols(mf.real), to_cols(mb.real), -to_cols(mf.imag), -to_cols(mb.imag)], axis=1)
    a16 = lam_pow[:, :, tc]
    a_re = jnp.concatenate([a16[0].real, a16[1].real], axis=-1)
    a_im = jnp.concatenate([a16[0].imag, a16[1].imag], axis=-1)
    return w_t.astype(BF16), w_z.astype(BF16), w_c.astype(BF16), a_re, a_im


def _s5_z_kernel(u_ref, w_ref, z_ref):
    z_ref[0] = _dot(u_ref[0], w_ref[0])


def s5_chunk_sums(u, w_z):
    bg, kc, cw = u.shape
    return pl.pallas_call(
        _s5_z_kernel,
        out_shape=jax.ShapeDtypeStruct((bg, kc, cw), F32),
        grid=(bg,),
        in_specs=[pl.BlockSpec((1, kc, cw), lambda i: (i, 0, 0)), pl.BlockSpec((1, cw, cw), lambda i: (i % SSM_GROUPS, 0, 0))],
        out_specs=pl.BlockSpec((1, kc, cw), lambda i: (i, 0, 0)),
        compiler_params=_cp(("parallel",)),
        name="s5_chunk_sums",
    )(u, w_z)


def _s5_scan_kernel(z_ref, ar_ref, ai_ref, h_ref, *, k_lat, k_ctx):
    ar, ai = ar_ref[...], ai_ref[...]
    rows = ar.shape[0]
    fwd_lanes = lax.broadcasted_iota(jnp.int32, (rows, LANE), 1) < SSM_STATE

    def step(cf, cb, hr, hi):
        h_ref[cf, :, 0:SSM_STATE] = hr[:, :SSM_STATE]
        h_ref[cb, :, SSM_STATE:LANE] = hr[:, SSM_STATE:]
        h_ref[cf, :, LANE:LANE + SSM_STATE] = hi[:, :SSM_STATE]
        h_ref[cb, :, LANE + SSM_STATE:] = hi[:, SSM_STATE:]
        zf, zb = z_ref[cf], z_ref[cb]
        zr = jnp.where(fwd_lanes, zf[:, :LANE], zb[:, :LANE])
        zi = jnp.where(fwd_lanes, zf[:, LANE:], zb[:, LANE:])
        return ar * hr - ai * hi + zr, ar * hi + ai * hr + zi

    def ctx_body(n, c):
        return step(k_lat + n, k_lat + k_ctx - 1 - n, *c)

    def lat_body(n, c):
        return step(n, k_lat - 1 - n, *c)

    zero = jnp.zeros((rows, LANE), F32)
    carry = lax.fori_loop(0, k_ctx, ctx_body, (zero, zero))
    lax.fori_loop(0, k_lat, lat_body, carry)


def s5_state_scan(z, a_re, a_im, k_lat, k_ctx):
    kc, r, cw = z.shape
    tr = 8
    return pl.pallas_call(
        functools.partial(_s5_scan_kernel, k_lat=k_lat, k_ctx=k_ctx),
        out_shape=jax.ShapeDtypeStruct((kc, r, cw), F32),
        grid=(r // tr,),
        in_specs=[pl.BlockSpec((kc, tr, cw), lambda i: (0, i, 0)), pl.BlockSpec((tr, LANE), lambda i: (i, 0)),
                  pl.BlockSpec((tr, LANE), lambda i: (i, 0))],
        out_specs=pl.BlockSpec((kc, tr, cw), lambda i: (0, i, 0)),
        compiler_params=_cp(("parallel",)),
        name="s5_state_scan",
    )(z, a_re, a_im)


def _gelu_tanh(y):
    return 0.5 * y * (1.0 + jnp.tanh(math.sqrt(2.0 / math.pi) * (y + 0.044715 * (y * y * y))))


def _s5_y_kernel(u_ref, h_ref, wt_ref, wc_ref, y_ref):
    y = _dot(u_ref[0], wt_ref[0]) + _dot(h_ref[0].astype(BF16), wc_ref[0])
    y_ref[0] = _gelu_tanh(y).astype(y_ref.dtype)


def s5_outputs(u, h, w_t, w_c):
    bg, kc, cw = u.shape
    blk = lambda: pl.BlockSpec((1, kc, cw), lambda i: (i, 0, 0))
    wsp = lambda: pl.BlockSpec((1, cw, cw), lambda i: (i % SSM_GROUPS, 0, 0))
    return pl.pallas_call(
        _s5_y_kernel,
        out_shape=jax.ShapeDtypeStruct((bg, kc, cw), BF16),
        grid=(bg,),
        in_specs=[blk(), blk(), wsp(), wsp()],
        out_specs=blk(),
        compiler_params=_cp(("parallel",)),
        name="s5_outputs",
    )(u, h, w_t, w_c)


def _glu_kernel(y_ref, w_ref, o_ref):
    z = _dot(y_ref[...], w_ref[...])
    half = z.shape[1] // 2
    o_ref[...] = (z[:, :half] * jax.nn.sigmoid(z[:, half:])).astype(o_ref.dtype)


def glu_matmul(y, w, tm):
    t, kdim = y.shape
    return pl.pallas_call(
        _glu_kernel,
        out_shape=jax.ShapeDtypeStruct((t, w.shape[1] // 2), BF16),
        grid=(t // tm,),
        in_specs=[pl.BlockSpec((tm, kdim), lambda i: (i, 0)), _const_spec(w.shape)],
        out_specs=pl.BlockSpec((tm, w.shape[1] // 2), lambda i: (i, 0)),
        compiler_params=_cp(("parallel",)),
        name="glu_matmul",
    )(y, w)


def s5_mixer(u_tok, tables, w_glu, n_batch, seq, ctx_len):
    w_t, w_z, w_c, a_re, a_im = tables
    tc = SSM_CHUNK
    n = seq + ctx_len
    kc, k_lat, k_ctx = n // tc, seq // tc, ctx_len // tc
    u_seq = jnp.concatenate([u_tok[:n_batch * seq].reshape(n_batch, seq, MIX_W),
                             u_tok[n_batch * seq:].reshape(n_batch, ctx_len, MIX_W)], axis=1)
    u_ch = u_seq.reshape(n_batch, kc, tc, SSM_GROUPS, SSM_CH).transpose(0, 3, 1, 2, 4).reshape(n_batch * SSM_GROUPS, kc, tc * SSM_CH)
    z = s5_chunk_sums(u_ch, w_z)
    h = s5_state_scan(z.transpose(1, 0, 2), jnp.tile(a_re, (n_batch, 1)), jnp.tile(a_im, (n_batch, 1)), k_lat, k_ctx)
    y = s5_outputs(u_ch, h.transpose(1, 0, 2), w_t, w_c)
    y = y.reshape(n_batch, SSM_GROUPS, kc, tc, SSM_CH).transpose(0, 2, 3, 1, 4).reshape(n_batch, n, MIX_W)
    y_tok = jnp.concatenate([y[:, :seq].reshape(n_batch * seq, MIX_W), y[:, seq:].reshape(n_batch * ctx_len, MIX_W)], axis=0)
    return glu_matmul(y_tok, w_glu, 512)


def _chan_dft_kernel(x_ref, w_ref, yc_ref, ys_ref):
    x = x_ref[...]
    for g in range(FNO_GROUPS):
        y = _dot(x[:, FNO_CH * g:FNO_CH * (g + 1)], w_ref[...])
        yc_ref[:, FNO_CH * g:FNO_CH * (g + 1)] = y[:, :FNO_CH].astype(BF16)
        ys_ref[:, FNO_CH * g:FNO_CH * (g + 1)] = y[:, FNO_CH:].astype(BF16)


def chan_dft(x, w, tm):
    t, cw = x.shape
    return pl.pallas_call(
        _chan_dft_kernel,
        out_shape=[jax.ShapeDtypeStruct((t, cw), BF16)] * 2,
        grid=(t // tm,),
        in_specs=[pl.BlockSpec((tm, cw), lambda i: (i, 0)), _const_spec(w.shape)],
        out_specs=[pl.BlockSpec((tm, cw), lambda i: (i, 0))] * 2,
        compiler_params=_cp(("parallel",)),
        name="chan_dft",
    )(x, w)


def _pos_dft_kernel(c_ref, s_ref, yc_ref, ys_ref, o_ref, acc_ref, *, n_k):
    j = pl.program_id(2)

    @pl.when(j == 0)
    def _():
        acc_ref[...] = jnp.zeros_like(acc_ref)

    acc_ref[...] += _dot(c_ref[...], yc_ref[0]) - _dot(s_ref[...], ys_ref[0])

    @pl.when(j == n_k - 1)
    def _():
        o_ref[0] = acc_ref[...].astype(o_ref.dtype)


def pos_dft(cos_t, sin_t, yc, ys, tm, tk):
    nb, n, w = yc.shape
    n_k = n // tk
    return pl.pallas_call(
        functools.partial(_pos_dft_kernel, n_k=n_k),
        out_shape=jax.ShapeDtypeStruct((nb, n, w), BF16),
        grid=(nb, n // tm, n_k),
        in_specs=[pl.BlockSpec((tm, tk), lambda b, i, j: (i, j)), pl.BlockSpec((tm, tk), lambda b, i, j: (i, j)),
                  pl.BlockSpec((1, tk, w), lambda b, i, j: (b, j, 0)), pl.BlockSpec((1, tk, w), lambda b, i, j: (b, j, 0))],
        out_specs=pl.BlockSpec((1, tm, w), lambda b, i, j: (b, i, 0)),
        scratch_shapes=[pltpu.VMEM((tm, w), F32)],
        compiler_params=_cp(("parallel", "parallel", "arbitrary")),
        name="pos_dft",
    )(cos_t, sin_t, yc, ys)


def dft_tables(n):
    idx = (jnp.arange(n, dtype=jnp.int32)[:, None] * jnp.arange(n, dtype=jnp.int32)[None, :]) % n
    ang = idx.astype(F32) * (2.0 * math.pi / n)
    return jnp.cos(ang).astype(BF16), jnp.sin(ang).astype(BF16)


def chan_table(n_pos):
    c = jnp.arange(FNO_CH, dtype=jnp.int32)
    ang = ((c[:, None] * c[None, :]) % FNO_CH).astype(F32) * (2.0 * math.pi / FNO_CH)
    scale = 1.0 / math.sqrt(n_pos * FNO_CH)
    return (jnp.concatenate([jnp.cos(ang), jnp.sin(ang)], axis=1) * scale).astype(BF16)


def fourier_mixer(x_seq, pos_tabs, tm, tk):
    nb, n, w = x_seq.shape
    yc, ys = chan_dft(x_seq.reshape(nb * n, w), chan_table(n), min(512, n))
    return pos_dft(pos_tabs[0], pos_tabs[1], yc.reshape(nb, n, w), ys.reshape(nb, n, w), tm, tk)


FFT_CH = 8


def fft_tables(n):
    a = n // LANE
    ia = jnp.arange(a, dtype=jnp.int32)
    ib = jnp.arange(LANE, dtype=jnp.int32)
    ang_a = ((ia[:, None] * ia[None, :]) % a).astype(F32) * (2.0 * math.pi / a)
    ang_b = ((ib[:, None] * ib[None, :]) % LANE).astype(F32) * (2.0 * math.pi / LANE)
    ang_t = (ia[:, None] * ib[None, :]).astype(F32) * (2.0 * math.pi / n)
    f_a = jnp.concatenate([jnp.cos(ang_a), jnp.sin(ang_a)], axis=0).astype(BF16)
    return f_a, jnp.cos(ang_b).astype(BF16), jnp.sin(ang_b).astype(BF16), jnp.cos(ang_t), -jnp.sin(ang_t)


def _fft_kernel(yc_ref, ys_ref, fa_ref, cb_ref, sb_ref, tr_ref, ti_ref, o_ref, *, a):
    fa, cb, sb, tr, ti = fa_ref[...], cb_ref[...], sb_ref[...], tr_ref[...], ti_ref[...]
    for j in range(FFT_CH):
        y = _dot(fa, jnp.concatenate([yc_ref[0, j], ys_ref[0, j]], axis=1))
        gr = y[:a, :LANE] - y[a:, LANE:]
        gi = -(y[:a, LANE:] + y[a:, :LANE])
        ar = (gr * tr - gi * ti).astype(BF16)
        ai = (gr * ti + gi * tr).astype(BF16)
        o_ref[0, j] = (_dot_nt(cb, ar) + _dot_nt(sb, ai)).astype(o_ref.dtype)


def fft_positions(yc, ys, tabs):
    nb, w, a, _ = yc.shape
    xin = lambda: pl.BlockSpec((1, FFT_CH, a, LANE), lambda b, c: (b, c, 0, 0))
    return pl.pallas_call(
        functools.partial(_fft_kernel, a=a),
        out_shape=jax.ShapeDtypeStruct((nb, w, LANE, a), BF16),
        grid=(nb, w // FFT_CH),
        in_specs=[xin(), xin()] + [_const_spec(t.shape) for t in tabs],
        out_specs=pl.BlockSpec((1, FFT_CH, LANE, a), lambda b, c: (b, c, 0, 0)),
        compiler_params=_cp(("parallel", "parallel")),
        name="fft_positions",
    )(yc, ys, *tabs)


def fourier_mixer_fft(x_seq, tabs):
    nb, n, w = x_seq.shape
    yc, ys = chan_dft(x_seq.reshape(nb * n, w), chan_table(n), min(512, n))
    chan_major = lambda y: y.reshape(nb, n, w).transpose(0, 2, 1).reshape(nb, w, n // LANE, LANE)
    z = fft_positions(chan_major(yc), chan_major(ys), tabs)
    return z.reshape(nb, w, n).transpose(0, 2, 1)


def _route(logits_t, bias_col):
    score = jax.nn.sigmoid(logits_t)
    sel = score + bias_col
    sa, sb, sc, sd = (sel[8 * i:8 * (i + 1)] for i in range(EXPERTS_PER_GROUP))
    ra, rb, rc, rd = (score[8 * i:8 * (i + 1)] for i in range(EXPERTS_PER_GROUP))
    m1, n1 = jnp.maximum(sa, sb), jnp.minimum(sa, sb)
    m2, n2 = jnp.maximum(sc, sd), jnp.minimum(sc, sd)
    gsum = jnp.maximum(m1, m2) + jnp.maximum(jnp.minimum(m1, m2), jnp.maximum(n1, n2))
    rows = lax.broadcasted_iota(jnp.int32, gsum.shape, 0)
    neg = jnp.float32(-jnp.inf)
    gsum = jnp.where(rows < N_EXPERT_GROUPS, gsum, neg)
    best = jnp.max(gsum, axis=0, keepdims=True)
    grp_f = jnp.min(jnp.where(gsum == best, rows.astype(F32), 8.0), axis=0, keepdims=True)
    grp = grp_f.astype(jnp.int32)
    hot = rows == grp
    pick = lambda v: jnp.sum(jnp.where(hot, v, 0.0), axis=0, keepdims=True)
    a, b, c, d = pick(sa), pick(sb), pick(sc), pick(sd)
    wa, wb, wc, wd = pick(ra), pick(rb), pick(rc), pick(rd)

    def first_max(va, vb, vc, vd):
        m = jnp.maximum(jnp.maximum(va, vb), jnp.maximum(vc, vd))
        return jnp.where(va == m, 0, jnp.where(vb == m, 1, jnp.where(vc == m, 2, 3)))

    i1 = first_max(a, b, c, d)
    i2 = first_max(jnp.where(i1 == 0, neg, a), jnp.where(i1 == 1, neg, b), jnp.where(i1 == 2, neg, c), jnp.where(i1 == 3, neg, d))
    gate = lambda i: jnp.where(i == 0, wa, jnp.where(i == 1, wb, jnp.where(i == 2, wc, wd)))
    g1, g2 = gate(i1), gate(i2)
    den = g1 + g2
    return grp * EXPERTS_PER_GROUP + i1, grp * EXPERTS_PER_GROUP + i2, g1 / den, g2 / den


def _outproj_kernel(pa_ref, pb_ref, pc_ref, pd_ref, x_ref, w_ref, g1_ref, gam_ref, sh_ref, sc_ref, rw_ref, rb_ref,
                    xo_ref, h_ref, e_ref, gt_ref):
    acc = _dot(pa_ref[...], w_ref[0:MIX_W])
    for i, p_ref in enumerate((pb_ref, pc_ref, pd_ref), start=1):
        acc += _dot(p_ref[...], w_ref[MIX_W * i:MIX_W * (i + 1)])
    x = x_ref[...] + g1_ref[0] * acc
    xo_ref[...] = x
    h = _rms(x, gam_ref[...]) * (1.0 + sc_ref[0]) + sh_ref[0]
    h_ref[...] = h.astype(BF16)
    logits = _dot3(h, rw_ref[...])
    e1, e2, g1, g2 = _route(logits.T[:32], rb_ref[...])
    rows = lax.broadcasted_iota(jnp.int32, e_ref.shape, 0)
    e_ref[...] = jnp.where(rows == 0, e1, jnp.where(rows == 1, e2, 0))
    gt_ref[...] = jnp.where(rows == 0, g1, jnp.where(rows == 1, g2, 0.0))


def out_projection(parts, tok, w_out, gate1, gamma, shift, scale, rw, rb, n_rows, lat_tiles_per_batch, n_batch, tm):
    d = tok.shape[1]
    mod_idx = lambda i: (jnp.minimum(i // lat_tiles_per_batch, n_batch), 0, 0)
    row = lambda w: pl.BlockSpec((tm, w), lambda i: (i, 0))
    mod = lambda: pl.BlockSpec((1, 1, d), mod_idx)
    return pl.pallas_call(
        _outproj_kernel,
        out_shape=[jax.ShapeDtypeStruct((n_rows, d), F32), jax.ShapeDtypeStruct((n_rows, d), BF16),
                   jax.ShapeDtypeStruct((8, n_rows), jnp.int32), jax.ShapeDtypeStruct((8, n_rows), F32)],
        grid=(n_rows // tm,),
        in_specs=[row(MIX_W), row(MIX_W), row(MIX_W), row(MIX_W), row(d),
                  pl.BlockSpec(w_out.shape, lambda i: (0, 0), pipeline_mode=pl.Buffered(1)),
                  mod(), _const_spec((1, d)), mod(), mod(), _const_spec(rw.shape), _const_spec(rb.shape)],
        out_specs=[row(d), row(d), pl.BlockSpec((8, tm), lambda i: (0, i)), pl.BlockSpec((8, tm), lambda i: (0, i))],
        compiler_params=_cp(("parallel",)),
        name="out_projection",
    )(*parts, tok, w_out, gate1, gamma.reshape(1, d), shift, scale, rw, rb)


def _moe_kernel(be_ref, nu_ref, x_ref, wg_ref, wu_ref, wd_ref, y_ref):
    i = pl.program_id(0)

    @pl.when(i < nu_ref[0])
    def _():
        x = x_ref[...]
        a = (jax.nn.silu(_dot(x, wg_ref[0])) * _dot(x, wu_ref[0])).astype(BF16)
        y_ref[...] = _dot(a, wd_ref[0])

    @pl.when(i >= nu_ref[0])
    def _():
        y_ref[...] = jnp.zeros_like(y_ref)


def moe_experts(xb, block_expert, n_used, wg, wu, wd):
    n_rows, d = xb.shape
    n_blocks = n_rows // MOE_BLOCK
    de = wg.shape[2]
    grid_spec = pltpu.PrefetchScalarGridSpec(
        num_scalar_prefetch=2,
        grid=(n_blocks,),
        in_specs=[pl.BlockSpec((MOE_BLOCK, d), lambda i, be, nu: (i, 0)),
                  pl.BlockSpec((1, d, de), lambda i, be, nu: (be[i], 0, 0)), pl.BlockSpec((1, d, de), lambda i, be, nu: (be[i], 0, 0)),
                  pl.BlockSpec((1, de, d), lambda i, be, nu: (be[i], 0, 0))],
        out_specs=pl.BlockSpec((MOE_BLOCK, d), lambda i, be, nu: (i, 0)),
    )
    return pl.pallas_call(
        _moe_kernel,
        out_shape=jax.ShapeDtypeStruct((n_rows, d), F32),
        grid_spec=grid_spec,
        compiler_params=_cp(("arbitrary",)),
        name="moe_experts",
    )(block_expert, n_used, xb, wg, wu, wd)


def routed_moe(h, eid, gate, wg, wu, wd):
    t = h.shape[0]
    bm = MOE_BLOCK
    flat_e = eid[:2].reshape(-1)
    flat_tok = jnp.tile(jnp.arange(t, dtype=jnp.int32), 2)
    hot = (flat_e[:, None] == jnp.arange(N_EXPERTS, dtype=jnp.int32)[None, :]).astype(jnp.int32)
    csum = jnp.cumsum(hot, axis=0)
    counts = csum[-1]
    padded = (counts + bm - 1) // bm * bm
    pad_end = jnp.cumsum(padded)
    pad_start = pad_end - padded
    dest = jnp.sum(hot * (pad_start[None, :] + csum - 1), axis=1)
    n_blocks = (2 * t + N_EXPERTS * (bm - 1) + bm - 1) // bm
    n_rows = n_blocks * bm
    row_tok = jnp.zeros((n_rows,), jnp.int32).at[dest].set(flat_tok)
    block_expert = jnp.minimum(jnp.searchsorted(pad_end, jnp.arange(n_blocks, dtype=jnp.int32) * bm, side='right'),
                               N_EXPERTS - 1).astype(jnp.int32)
    n_used = (pad_end[-1] // bm).astype(jnp.int32).reshape(1)
    yb = moe_experts(h[row_tok], block_expert, n_used, wg, wu, wd)
    pos = dest.reshape(2, t)
    return yb[pos[0]], yb[pos[1]], gate[0].reshape(t, 1), gate[1].reshape(t, 1)


def _moe_residual(x_ref, y0_ref, y1_ref, w0_ref, w1_ref, g_ref):
    return x_ref[...] + g_ref[0] * (w0_ref[...] * y0_ref[...] + w1_ref[...] * y1_ref[...])


def _residual_kernel(x_ref, y0_ref, y1_ref, w0_ref, w1_ref, g_ref, o_ref):
    o_ref[...] = _moe_residual(x_ref, y0_ref, y1_ref, w0_ref, w1_ref, g_ref)


def _final_kernel(x_ref, y0_ref, y1_ref, w0_ref, w1_ref, g_ref, gam_ref, o_ref):
    o_ref[...] = _rms(_moe_residual(x_ref, y0_ref, y1_ref, w0_ref, w1_ref, g_ref), gam_ref[...])


def gated_residual(x, moe_out, gate, lat_tiles_per_batch, n_batch, tm, final_gamma=None):
    t, d = x.shape
    mod_idx = lambda i: (jnp.minimum(i // lat_tiles_per_batch, n_batch), 0, 0)
    row = lambda: pl.BlockSpec((tm, d), lambda i: (i, 0))
    col = lambda: pl.BlockSpec((tm, 1), lambda i: (i, 0))
    in_specs = [row(), row(), row(), col(), col(), pl.BlockSpec((1, 1, d), mod_idx)]
    args = [x, *moe_out, gate]
    body = _residual_kernel
    if final_gamma is not None:
        in_specs.append(_const_spec((1, d)))
        args.append(final_gamma.reshape(1, d))
        body = _final_kernel
    return pl.pallas_call(
        body,
        out_shape=jax.ShapeDtypeStruct((t, d), F32),
        grid=(t // tm,),
        in_specs=in_specs,
        out_specs=row(),
        compiler_params=_cp(("parallel",)),
        name="gated_residual",
    )(*args)


def _rope_tables(seq, ctx_len):
    pos = jnp.arange(seq)
    row = (pos // GRID_W).astype(F32)
    col = (pos % GRID_W).astype(F32)

    def tab(dim):
        quarter = dim // 4
        inv = ROPE_THETA ** (-jnp.arange(quarter, dtype=F32) / quarter)
        ang = jnp.concatenate([row[:, None] * inv, col[:, None] * inv], axis=-1)
        cos, sin = jnp.cos(ang), jnp.sin(ang)
        cos_f = jnp.concatenate([cos, cos], axis=-1)
        sin_f = jnp.concatenate([-sin, sin], axis=-1)
        cos_f = jnp.concatenate([cos_f, jnp.ones((ctx_len, dim), F32)], axis=0)
        sin_f = jnp.concatenate([sin_f, jnp.zeros((ctx_len, dim), F32)], axis=0)
        pad = LANE - dim
        return jnp.pad(cos_f, ((0, 0), (0, pad))), jnp.pad(sin_f, ((0, 0), (0, pad)))

    cm, sm = tab(MLA_ROPE)
    cg, sg = tab(GQA_HD)
    return cm, sm, cg, sg


def _half_swap(w, dim):
    return jnp.concatenate([w[..., dim // 2:], w[..., :dim // 2]], axis=-1)


def _layout_w_in(w):
    d = w.shape[0]
    k_rot = w[:, MLA_Q_RANK + MLA_KV_RANK:MLA_IN]
    z = jnp.zeros((d, LANE - MLA_ROPE), w.dtype)
    return jnp.concatenate([w[:, :MLA_Q_RANK + MLA_KV_RANK], k_rot, z, _half_swap(k_rot, MLA_ROPE), z, w[:, OFF_B:]], axis=1).astype(BF16)


def _layout_w_uq(w):
    r = w.shape[0]
    wh = w.reshape(r, MLA_HEADS, MLA_NOPE + MLA_ROPE)
    rope = wh[..., MLA_NOPE:]
    z = jnp.zeros((r, MLA_HEADS, LANE - MLA_ROPE), w.dtype)
    main = jnp.concatenate([wh[..., :MLA_NOPE], rope, z], axis=-1).reshape(r, MLA_HEADS * MLA_DQ)
    rot = jnp.concatenate([_half_swap(rope, MLA_ROPE), z], axis=-1).reshape(r, MLA_HEADS * LANE)
    return jnp.concatenate([main, rot], axis=1).astype(BF16)


def _layout_router(router_w, router_bias):
    d = router_w.shape[0]
    w4 = router_w.reshape(d, N_EXPERT_GROUPS, EXPERTS_PER_GROUP).transpose(0, 2, 1)
    w = jnp.pad(w4, ((0, 0), (0, 0), (0, 8 - N_EXPERT_GROUPS))).reshape(d, 8 * EXPERTS_PER_GROUP)
    w = jnp.pad(w, ((0, 0), (0, LANE - 8 * EXPERTS_PER_GROUP))).astype(F32)
    b4 = router_bias.astype(F32).reshape(N_EXPERT_GROUPS, EXPERTS_PER_GROUP).T
    b = jnp.pad(b4, ((0, 0), (0, 8 - N_EXPERT_GROUPS))).reshape(8 * EXPERTS_PER_GROUP, 1)
    return w, b


def kernel(x, c, ctx, c_ctx, ada_w, ada_b, norm1_g, norm2_g, w_in, mla_q_norm_g, mla_w_uq, mla_kv_norm_g, mla_w_ukv, gqa_q_norm_g, gqa_k_norm_g, ssm_lam_re, ssm_lam_im, ssm_log_dt, ssm_b_re, ssm_b_im, ssm_c_re, ssm_c_im, ssm_d, ssm_w_glu, w_out, router_w, router_bias, moe_w_gate, moe_w_up, moe_w_down, final_norm_g):
    nb, seq, d = x.shape
    ctx_len = ctx.shape[1]
    depth = ada_w.shape[0]
    n_lat, n_ctx = nb * seq, nb * ctx_len
    tm = min(512, n_ctx)
    lat_tiles = seq // tm
    rows_q = min(1024, seq)
    tk_lat = min(2048, seq)
    in_widths = (768, GQA_IN, MIX_W, MIX_W)

    tok = jnp.concatenate([x.reshape(n_lat, d), ctx.reshape(n_ctx, d)], axis=0)
    silu = jnp.concatenate([jax.nn.silu(c), jnp.broadcast_to(jax.nn.silu(c_ctx)[None], (8 - nb, d))], axis=0)
    rope_tabs = _rope_tables(seq, ctx_len)
    rw, rb = _layout_router(router_w, router_bias)
    fft_lat = fft_tables(seq)
    dft_ctx = dft_tables(ctx_len)

    for l in range(depth):
        need_ctx = l < depth - 1
        mod = mod_vectors(silu, ada_w[l], ada_b[l])[:nb + 1].reshape(nb + 1, 1, 6, d)
        sh1, sc1, g1, sh2, sc2, g2 = (mod[:, :, i] for i in range(6))

        pa, pb, pc, pd = in_projection(tok, norm1_g[l], sh1, sc1, _layout_w_in(w_in[l]), in_widths, lat_tiles, nb, tm)

        qm, km, vm, qg, kg, vg = qkv_prepare(pa, pb, rope_tabs, mla_q_norm_g[l], mla_kv_norm_g[l], gqa_q_norm_g[l],
                                             gqa_k_norm_g[l], _layout_w_uq(mla_w_uq[l]), mla_w_ukv[l].astype(BF16),
                                             nb, seq, ctx_len, min(256, ctx_len))
        qm5 = qm.reshape(nb, MLA_HEADS, 1, seq + ctx_len, MLA_DQ)
        out_a = flash_attention(qm5, km, vm, seq, ctx_len, rows_q, tk_lat).reshape(n_lat, MIX_W)
        out_b = flash_attention(qg, kg, vg, seq, ctx_len, rows_q // qg.shape[2], tk_lat).reshape(n_lat, MIX_W)

        s5_tabs = s5_tables(ssm_lam_re[l], ssm_lam_im[l], ssm_log_dt[l], ssm_b_re[l], ssm_b_im[l], ssm_c_re[l], ssm_c_im[l], ssm_d[l])
        out_c = s5_mixer(pc, s5_tabs, ssm_w_glu[l].astype(BF16), nb, seq, ctx_len)

        out_d = fourier_mixer_fft(pd[:n_lat].reshape(nb, seq, MIX_W), fft_lat).reshape(n_lat, MIX_W)

        if need_ctx:
            ctx_a = context_attention(qm5, km, vm, seq, ctx_len).reshape(n_ctx, MIX_W)
            ctx_b = context_attention(qg, kg, vg, seq, ctx_len).reshape(n_ctx, MIX_W)
            ctx_d = fourier_mixer(pd[n_lat:].reshape(nb, ctx_len, MIX_W), dft_ctx, ctx_len, ctx_len).reshape(n_ctx, MIX_W)
            parts = [jnp.concatenate([out_a, ctx_a], axis=0), jnp.concatenate([out_b, ctx_b], axis=0), out_c,
                     jnp.concatenate([out_d, ctx_d], axis=0)]
            n_rows = n_lat + n_ctx
        else:
            parts = [out_a, out_b, out_c, out_d]
            n_rows = n_lat

        tok, h2, eid, gate = out_projection(parts, tok, w_out[l].astype(BF16), g1, norm2_g[l], sh2, sc2, rw, rb,
                                            n_rows, 2 * lat_tiles, nb, tm // 2)
        y = routed_moe(h2, eid, gate, moe_w_gate[l].astype(BF16), moe_w_up[l].astype(BF16), moe_w_down[l].astype(BF16))
        tok = gated_residual(tok, y, g2, lat_tiles, nb, tm, None if need_ctx else final_norm_g)

    return tok[:n_lat].reshape(nb, seq, d)
```

```python
import functools
import math

import jax
import jax.numpy as jnp
from jax import lax
from jax.experimental import pallas as pl
from jax.experimental.pallas import tpu as pltpu

F32 = jnp.float32
BF16 = jnp.bfloat16

D_MODEL = 2048
GRID_W = 64
ROPE_THETA = 10000.0
RMS_EPS = 1e-6
MIX_W = D_MODEL // 4
MLA_HEADS = 4
MLA_NOPE = 128
MLA_V = 128
MLA_ROPE = 64
MLA_Q_RANK = 384
MLA_KV_RANK = 128
MLA_IN = MLA_Q_RANK + MLA_KV_RANK + MLA_ROPE
MLA_DQ = 256
GQA_HEADS = 4
GQA_KV_HEADS = 2
GQA_HD = 128
GQA_IN = (GQA_HEADS + 2 * GQA_KV_HEADS) * GQA_HD
SSM_GROUPS = 32
SSM_CH = 16
SSM_STATE = 64
SSM_CHUNK = 16
FNO_GROUPS = 4
FNO_CH = 128
OFF_B = MLA_IN
OFF_C = OFF_B + GQA_IN
OFF_D = OFF_C + MIX_W
N_EXPERTS = 16
N_EXPERT_GROUPS = 4
EXPERTS_PER_GROUP = 4
D_EXPERT = D_MODEL // 2
MOE_BLOCK = 256
LANE = 128
MXU_W = 256
LOG2E = 1.4426950408889634
VMEM_LIMIT = 52 * 1024 * 1024


def _cp(sem, vmem=VMEM_LIMIT):
    return pltpu.CompilerParams(dimension_semantics=sem, vmem_limit_bytes=vmem)


def _const_spec(shape):
    nd = len(shape)
    return pl.BlockSpec(shape, lambda *_: (0,) * nd)


def _split_bf16(a):
    hi = a.astype(BF16)
    lo = (a - hi.astype(F32)).astype(BF16)
    return hi, lo


def _dot(a, b):
    return jnp.dot(a, b, preferred_element_type=F32)


def _dot3(a, b):
    ah, al = _split_bf16(a)
    bh, bl = _split_bf16(b)
    return _dot(ah, bh) + _dot(al, bh) + _dot(ah, bl)


def _rms(x, g):
    return x * lax.rsqrt(jnp.mean(x * x, axis=-1, keepdims=True) + RMS_EPS) * g


def _cast_kernel(x_ref, o_ref):
    o_ref[...] = x_ref[...].astype(o_ref.dtype)


def cast_bf16(w, rows_per_block=1024):
    c = w.shape[-1]
    r = w.size // c
    out = pl.pallas_call(
        _cast_kernel,
        out_shape=jax.ShapeDtypeStruct((r, c), BF16),
        grid=(r // rows_per_block,),
        in_specs=[pl.BlockSpec((rows_per_block, c), lambda i: (i, 0))],
        out_specs=pl.BlockSpec((rows_per_block, c), lambda i: (i, 0)),
        compiler_params=_cp(("parallel",)),
        name="cast_bf16",
    )(w.reshape(r, c))
    return out.reshape(w.shape)


def _mod_kernel(s_ref, w_ref, b_ref, o_ref):
    o_ref[...] = _dot3(s_ref[...], w_ref[...]) + b_ref[...]


def mod_vectors(s8, w, b):
    k, n = w.shape
    tn = 1024
    return pl.pallas_call(
        _mod_kernel,
        out_shape=jax.ShapeDtypeStruct((8, n), F32),
        grid=(n // tn,),
        in_specs=[_const_spec((8, k)), pl.BlockSpec((k, tn), lambda j: (0, j)), pl.BlockSpec((1, tn), lambda j: (0, j))],
        out_specs=pl.BlockSpec((8, tn), lambda j: (0, j)),
        compiler_params=_cp(("parallel",)),
        name="mod_vectors",
    )(s8, w, b.reshape(1, n))


def _inproj_kernel(x_ref, g_ref, sh_ref, sc_ref, w_ref, *o_refs, widths):
    h = _rms(x_ref[...], g_ref[...])
    hb = (h * (1.0 + sc_ref[0]) + sh_ref[0]).astype(BF16)
    off = 0
    for o_ref, wd in zip(o_refs, widths):
        o_ref[...] = _dot(hb, w_ref[:, off:off + wd]).astype(o_ref.dtype)
        off += wd


def in_projection(tok, gamma, shift, scale, w, widths, lat_tiles_per_batch, n_batch, tm):
    t, d = tok.shape
    mod_idx = lambda i: (jnp.minimum(i // lat_tiles_per_batch, n_batch), 0, 0)
    return pl.pallas_call(
        functools.partial(_inproj_kernel, widths=widths),
        out_shape=[jax.ShapeDtypeStruct((t, wd), BF16) for wd in widths],
        grid=(t // tm,),
        in_specs=[pl.BlockSpec((tm, d), lambda i: (i, 0)), _const_spec((1, d)),
                  pl.BlockSpec((1, 1, d), mod_idx), pl.BlockSpec((1, 1, d), mod_idx),
                  pl.BlockSpec(w.shape, lambda i: (0, 0), pipeline_mode=pl.Buffered(1))],
        out_specs=[pl.BlockSpec((tm, wd), lambda i: (i, 0)) for wd in widths],
        compiler_params=_cp(("parallel",)),
        name="in_projection",
    )(tok, gamma.reshape(1, d), shift, scale, w)


def _qkv_kernel(a_ref, b_ref, cm_ref, sm_ref, cg_ref, sg_ref, gq_ref, gkv_ref, ggq_ref, ggk_ref, wuq_ref, wukv_ref,
                qm_ref, km_ref, vm_ref, qg_ref, kg_ref, vg_ref):
    a = a_ref[...].astype(F32)
    cm, sm = cm_ref[...], sm_ref[...]
    mla_scale = LOG2E * (MLA_NOPE + MLA_ROPE) ** -0.5
    qn = _rms(a[:, :MLA_Q_RANK], gq_ref[...]).astype(BF16)
    q_all = _dot(qn, wuq_ref[...])
    rot0 = MLA_HEADS * MLA_DQ
    for h in range(MLA_HEADS):
        nope = q_all[:, MLA_DQ * h:MLA_DQ * h + LANE]
        rp = q_all[:, MLA_DQ * h + LANE:MLA_DQ * (h + 1)]
        rr = q_all[:, rot0 + LANE * h:rot0 + LANE * (h + 1)]
        qm_ref[0, h, :, :LANE] = (nope * mla_scale).astype(BF16)
        qm_ref[0, h, :, LANE:] = ((rp * cm + rr * sm) * mla_scale).astype(BF16)
    kvn = _rms(a[:, MLA_Q_RANK:MLA_Q_RANK + MLA_KV_RANK], gkv_ref[...]).astype(BF16)
    kv = _dot(kvn, wukv_ref[...])
    kr = (a[:, 512:640] * cm + a[:, 640:768] * sm).astype(BF16)
    for h in range(MLA_HEADS):
        km_ref[0, h, :, :LANE] = kv[:, 256 * h:256 * h + LANE].astype(BF16)
        km_ref[0, h, :, LANE:] = kr
        vm_ref[0, h] = kv[:, 256 * h + LANE:256 * (h + 1)].T.astype(BF16)
    b = b_ref[...].astype(F32)
    cg, sg = cg_ref[...], sg_ref[...]
    gqa_scale = LOG2E * GQA_HD ** -0.5
    grp = GQA_HEADS // GQA_KV_HEADS
    for h in range(GQA_HEADS):
        qh = _rms(b[:, GQA_HD * h:GQA_HD * (h + 1)], ggq_ref[...])
        qh = qh * cg + pltpu.roll(qh, GQA_HD // 2, 1) * sg
        qg_ref[0, h // grp, h % grp] = (qh * gqa_scale).astype(BF16)
    k0 = GQA_HEADS * GQA_HD
    v0 = k0 + GQA_KV_HEADS * GQA_HD
    for h in range(GQA_KV_HEADS):
        kh = _rms(b[:, k0 + GQA_HD * h:k0 + GQA_HD * (h + 1)], ggk_ref[...])
        kg_ref[0, h] = (kh * cg + pltpu.roll(kh, GQA_HD // 2, 1) * sg).astype(BF16)
        vg_ref[0, h] = b[:, v0 + GQA_HD * h:v0 + GQA_HD * (h + 1)].T.astype(BF16)

def qkv_prepare(pa, pb, tabs, g_q, g_kv, g_gq, g_gk, w_uq, w_ukv, n_batch, seq, ctx_len, tm):
    t = pa.shape[0]
    n = seq + ctx_len
    lat_tiles = seq // tm
    ctx_tiles = ctx_len // tm
    n_lat = n_batch * lat_tiles

    def bidx(i):
        j = i - n_lat
        bi = jnp.where(i < n_lat, i // lat_tiles, j // ctx_tiles)
        blk = jnp.where(i < n_lat, i % lat_tiles, lat_tiles + j % ctx_tiles)
        return bi, blk

    def o4(i):
        bi, blk = bidx(i)
        return (bi, 0, blk, 0)

    def o4t(i):
        bi, blk = bidx(i)
        return (bi, 0, 0, blk)

    def o5(i):
        bi, blk = bidx(i)
        return (bi, 0, 0, blk, 0)

    tab_spec = pl.BlockSpec((tm, LANE), lambda i: (bidx(i)[1], 0))
    vec = lambda w: _const_spec((1, w))
    return pl.pallas_call(
        _qkv_kernel,
        out_shape=[jax.ShapeDtypeStruct((n_batch, MLA_HEADS, n, MLA_DQ), BF16),
                   jax.ShapeDtypeStruct((n_batch, MLA_HEADS, n, MLA_DQ), BF16),
                   jax.ShapeDtypeStruct((n_batch, MLA_HEADS, MLA_V, n), BF16),
                   jax.ShapeDtypeStruct((n_batch, GQA_KV_HEADS, GQA_HEADS // GQA_KV_HEADS, n, GQA_HD), BF16),
                   jax.ShapeDtypeStruct((n_batch, GQA_KV_HEADS, n, GQA_HD), BF16),
                   jax.ShapeDtypeStruct((n_batch, GQA_KV_HEADS, GQA_HD, n), BF16)],
        grid=(t // tm,),
        in_specs=[pl.BlockSpec((tm, pa.shape[1]), lambda i: (i, 0)), pl.BlockSpec((tm, pb.shape[1]), lambda i: (i, 0)),
                  tab_spec, tab_spec, tab_spec, tab_spec,
                  vec(MLA_Q_RANK), vec(MLA_KV_RANK), vec(GQA_HD), vec(GQA_HD),
                  _const_spec(w_uq.shape), _const_spec(w_ukv.shape)],
        out_specs=[pl.BlockSpec((1, MLA_HEADS, tm, MLA_DQ), o4), pl.BlockSpec((1, MLA_HEADS, tm, MLA_DQ), o4),
                   pl.BlockSpec((1, MLA_HEADS, MLA_V, tm), o4t),
                   pl.BlockSpec((1, GQA_KV_HEADS, GQA_HEADS // GQA_KV_HEADS, tm, GQA_HD), o5),
                   pl.BlockSpec((1, GQA_KV_HEADS, tm, GQA_HD), o4), pl.BlockSpec((1, GQA_KV_HEADS, GQA_HD, tm), o4t)],
        compiler_params=_cp(("parallel",)),
        name="qkv_prepare",
    )(pa, pb, *tabs, g_q.reshape(1, -1), g_kv.reshape(1, -1), g_gq.reshape(1, -1), g_gk.reshape(1, -1), w_uq, w_ukv)


def _dot_nt(a, b):
    return lax.dot_general(a, b, (((1,), (1,)), ((), ())), preferred_element_type=F32)


def _softmax_cols(s):
    m = jnp.max(s, axis=0, keepdims=True)
    p = jnp.exp2(s - m)
    return m, p, jnp.sum(p, axis=0, keepdims=True)


def _store_heads(o_ref, o_t, grp, tq, dv):
    o = o_t.T
    for g in range(grp):
        o_ref[0, :, dv * g:dv * (g + 1)] = o[tq * g:tq * (g + 1)].astype(o_ref.dtype)


def _flash_kernel(q_ref, k_ref, vt_ref, kc_ref, vtc_ref, o_ref, m_sc, l_sc, acc_sc, *, grp, tq, dv, n_kv):
    j = pl.program_id(3)
    rows = grp * tq
    q = q_ref[0, 0].reshape(rows, q_ref.shape[-1])

    @pl.when(j == 0)
    def _():
        m, p, l = _softmax_cols(_dot_nt(kc_ref[0, 0], q))
        m_sc[...] = m
        l_sc[...] = l
        acc_sc[...] = _dot(vtc_ref[0, 0], p.astype(BF16))

    k = k_ref[0, 0]
    vt = vt_ref[0, 0]
    slabs = [slice(MXU_W * r, MXU_W * (r + 1)) for r in range(rows // MXU_W)]
    scores = [_dot_nt(k, q[sl]) for sl in slabs]
    for sl, s in zip(slabs, scores):
        m_prev = m_sc[:, sl]
        m_new = jnp.maximum(m_prev, jnp.max(s, axis=0, keepdims=True))
        alpha = jnp.exp2(m_prev - m_new)
        p = jnp.exp2(s - m_new)
        l_sc[:, sl] = alpha * l_sc[:, sl] + jnp.sum(p, axis=0, keepdims=True)
        acc_sc[:, sl] = alpha * acc_sc[:, sl] + _dot(vt, p.astype(BF16))
        m_sc[:, sl] = m_new

    @pl.when(j == n_kv - 1)
    def _():
        _store_heads(o_ref, acc_sc[...] / l_sc[...], grp, tq, dv)


def flash_attention(q, k, vt, seq, ctx_len, tq, tk):
    nb, hk, grp, n, dq = q.shape
    dv = vt.shape[-2]
    n_kv = seq // tk
    cb = seq // ctx_len
    rows = grp * tq
    return pl.pallas_call(
        functools.partial(_flash_kernel, grp=grp, tq=tq, dv=dv, n_kv=n_kv),
        out_shape=jax.ShapeDtypeStruct((nb, seq, hk * grp * dv), BF16),
        grid=(nb, hk, seq // tq, n_kv),
        in_specs=[pl.BlockSpec((1, 1, grp, tq, dq), lambda b, h, i, j: (b, h, 0, i, 0)),
                  pl.BlockSpec((1, 1, tk, dq), lambda b, h, i, j: (b, h, j, 0)),
                  pl.BlockSpec((1, 1, dv, tk), lambda b, h, i, j: (b, h, 0, j)),
                  pl.BlockSpec((1, 1, ctx_len, dq), lambda b, h, i, j: (b, h, cb, 0)),
                  pl.BlockSpec((1, 1, dv, ctx_len), lambda b, h, i, j: (b, h, 0, cb))],
        out_specs=pl.BlockSpec((1, tq, grp * dv), lambda b, h, i, j: (b, i, h)),
        scratch_shapes=[pltpu.VMEM((1, rows), F32), pltpu.VMEM((1, rows), F32), pltpu.VMEM((dv, rows), F32)],
        compiler_params=_cp(("parallel", "parallel", "parallel", "arbitrary")),
        name="flash_attention",
    )(q, k, vt, k, vt)


def _ctx_attn_kernel(q_ref, kc_ref, vtc_ref, o_ref, *, grp, tq, dv):
    q = q_ref[0, 0].reshape(grp * tq, q_ref.shape[-1])
    _, p, l = _softmax_cols(_dot_nt(kc_ref[0, 0], q))
    _store_heads(o_ref, _dot(vtc_ref[0, 0], p.astype(BF16)) / l, grp, tq, dv)


def context_attention(q, k, vt, seq, ctx_len):
    nb, hk, grp, n, dq = q.shape
    dv = vt.shape[-2]
    cb = seq // ctx_len
    return pl.pallas_call(
        functools.partial(_ctx_attn_kernel, grp=grp, tq=ctx_len, dv=dv),
        out_shape=jax.ShapeDtypeStruct((nb, ctx_len, hk * grp * dv), BF16),
        grid=(nb, hk),
        in_specs=[pl.BlockSpec((1, 1, grp, ctx_len, dq), lambda b, h: (b, h, 0, cb, 0)),
                  pl.BlockSpec((1, 1, ctx_len, dq), lambda b, h: (b, h, cb, 0)),
                  pl.BlockSpec((1, 1, dv, ctx_len), lambda b, h: (b, h, 0, cb))],
        out_specs=pl.BlockSpec((1, ctx_len, grp * dv), lambda b, h: (b, 0, h)),
        compiler_params=_cp(("parallel", "parallel")),
        name="context_attention",
    )(q, k, vt)


def s5_tables(lam_re, lam_im, log_dt, b_re, b_im, c_re, c_im, d):
    hp = lax.Precision.HIGHEST
    tc = SSM_CHUNK
    lam = lax.complex(lam_re.astype(F32), lam_im.astype(F32))
    dt = jnp.exp(log_dt.astype(F32))[..., None]
    lam_bar = jnp.exp(lam * dt)
    b_bar = ((lam_bar - 1.0) / lam)[..., None] * lax.complex(b_re.astype(F32), b_im.astype(F32))
    c_mat = lax.complex(c_re.astype(F32), c_im.astype(F32))
    steps = jnp.arange(tc + 1, dtype=F32)
    lam_pow = jnp.exp((lam * dt)[:, :, None, :] * steps[None, None, :, None])
    kern = jnp.einsum('dgip,dgtp,dgpj->dgtij', c_mat, lam_pow[:, :, :tc], b_bar, precision=hp).real
    s_idx = jnp.arange(tc)[:, None]
    t_idx = jnp.arange(tc)[None, :]
    kf = kern[0][:, jnp.clip(t_idx - s_idx, 0, tc - 1)] * (t_idx >= s_idx)[None, :, :, None, None]
    kb = kern[1][:, jnp.clip(s_idx - t_idx, 0, tc - 1)] * (s_idx >= t_idx)[None, :, :, None, None]
    dg = d.astype(F32).reshape(SSM_GROUPS, SSM_CH)
    skip = (s_idx == t_idx)[None, :, :, None, None] * (jnp.eye(SSM_CH, dtype=F32) * dg[:, :, None])[:, None, None]
    w_t = (kf + kb + skip).transpose(0, 1, 4, 2, 3).reshape(SSM_GROUPS, tc * SSM_CH, tc * SSM_CH)
    zf = lam_pow[0][:, tc - 1 - jnp.arange(tc)][..., None] * b_bar[0][:, None]
    zb = lam_pow[1][:, jnp.arange(tc)][..., None] * b_bar[1][:, None]
    to_rows = lambda z: z.transpose(0, 1, 3, 2).reshape(SSM_GROUPS, tc * SSM_CH, SSM_STATE)
    w_z = jnp.concatenate([to_rows(zf.real), to_rows(zb.real), to_rows(zf.imag), to_rows(zb.imag)], axis=-1)
    mf = c_mat[0][:, None] * lam_pow[0][:, 1 + jnp.arange(tc)][:, :, None, :]
    mb = c_mat[1][:, None] * lam_pow[1][:, tc - jnp.arange(tc)][:, :, None, :]
    to_cols = lambda m: m.transpose(0, 3, 1, 2).reshape(SSM_GROUPS, SSM_STATE, tc * SSM_CH)
    w_c = jnp.concatenate([to_cols(mf.real), to_cols(mb.real), -to_cols(mf.imag), -to_cols(mb.imag)], axis=1)
    a16 = lam_pow[:, :, tc]
    a_re = jnp.concatenate([a16[0].real, a16[1].real], axis=-1)
    a_im = jnp.concatenate([a16[0].imag, a16[1].imag], axis=-1)
    return w_t.astype(BF16), w_z.astype(BF16), w_c.astype(BF16), a_re, a_im


def _s5_z_kernel(u_ref, w_ref, z_ref):
    z_ref[0] = _dot(u_ref[0], w_ref[0])


def s5_chunk_sums(u, w_z):
    bg, kc, cw = u.shape
    return pl.pallas_call(
        _s5_z_kernel,
        out_shape=jax.ShapeDtypeStruct((bg, kc, cw), F32),
        grid=(bg,),
        in_specs=[pl.BlockSpec((1, kc, cw), lambda i: (i, 0, 0)), pl.BlockSpec((1, cw, cw), lambda i: (i % SSM_GROUPS, 0, 0))],
        out_specs=pl.BlockSpec((1, kc, cw), lambda i: (i, 0, 0)),
        compiler_params=_cp(("parallel",)),
        name="s5_chunk_sums",
    )(u, w_z)


def _s5_scan_kernel(z_ref, ar_ref, ai_ref, h_ref, *, k_lat, k_ctx):
    ar, ai = ar_ref[...], ai_ref[...]
    rows = ar.shape[0]
    fwd_lanes = lax.broadcasted_iota(jnp.int32, (rows, LANE), 1) < SSM_STATE

    def step(cf, cb, hr, hi):
        h_ref[cf, :, 0:SSM_STATE] = hr[:, :SSM_STATE]
        h_ref[cb, :, SSM_STATE:LANE] = hr[:, SSM_STATE:]
        h_ref[cf, :, LANE:LANE + SSM_STATE] = hi[:, :SSM_STATE]
        h_ref[cb, :, LANE + SSM_STATE:] = hi[:, SSM_STATE:]
        zf, zb = z_ref[cf], z_ref[cb]
        zr = jnp.where(fwd_lanes, zf[:, :LANE], zb[:, :LANE])
        zi = jnp.where(fwd_lanes, zf[:, LANE:], zb[:, LANE:])
        return ar * hr - ai * hi + zr, ar * hi + ai * hr + zi

    def ctx_body(n, c):
        return step(k_lat + n, k_lat + k_ctx - 1 - n, *c)

    def lat_body(n, c):
        return step(n, k_lat - 1 - n, *c)

    zero = jnp.zeros((rows, LANE), F32)
    carry = lax.fori_loop(0, k_ctx, ctx_body, (zero, zero))
    lax.fori_loop(0, k_lat, lat_body, carry)


def s5_state_scan(z, a_re, a_im, k_lat, k_ctx):
    kc, r, cw = z.shape
    tr = 8
    return pl.pallas_call(
        functools.partial(_s5_scan_kernel, k_lat=k_lat, k_ctx=k_ctx),
        out_shape=jax.ShapeDtypeStruct((kc, r, cw), F32),
        grid=(r // tr,),
        in_specs=[pl.BlockSpec((kc, tr, cw), lambda i: (0, i, 0)), pl.BlockSpec((tr, LANE), lambda i: (i, 0)),
                  pl.BlockSpec((tr, LANE), lambda i: (i, 0))],
        out_specs=pl.BlockSpec((kc, tr, cw), lambda i: (0, i, 0)),
        compiler_params=_cp(("parallel",)),
        name="s5_state_scan",
    )(z, a_re, a_im)


def _gelu_tanh(y):
    return 0.5 * y * (1.0 + jnp.tanh(math.sqrt(2.0 / math.pi) * (y + 0.044715 * (y * y * y))))


def _s5_y_kernel(u_ref, h_ref, wt_ref, wc_ref, y_ref):
    y = _dot(u_ref[0], wt_ref[0]) + _dot(h_ref[0].astype(BF16), wc_ref[0])
    y_ref[0] = _gelu_tanh(y).astype(y_ref.dtype)


def s5_outputs(u, h, w_t, w_c):
    bg, kc, cw = u.shape
    blk = lambda: pl.BlockSpec((1, kc, cw), lambda i: (i, 0, 0))
    wsp = lambda: pl.BlockSpec((1, cw, cw), lambda i: (i % SSM_GROUPS, 0, 0))
    return pl.pallas_call(
        _s5_y_kernel,
        out_shape=jax.ShapeDtypeStruct((bg, kc, cw), BF16),
        grid=(bg,),
        in_specs=[blk(), blk(), wsp(), wsp()],
        out_specs=blk(),
        compiler_params=_cp(("parallel",)),
        name="s5_outputs",
    )(u, h, w_t, w_c)


def _glu_kernel(y_ref, w_ref, o_ref):
    z = _dot(y_ref[...], w_ref[...])
    half = z.shape[1] // 2
    o_ref[...] = (z[:, :half] * jax.nn.sigmoid(z[:, half:])).astype(o_ref.dtype)


def glu_matmul(y, w, tm):
    t, kdim = y.shape
    return pl.pallas_call(
        _glu_kernel,
        out_shape=jax.ShapeDtypeStruct((t, w.shape[1] // 2), BF16),
        grid=(t // tm,),
        in_specs=[pl.BlockSpec((tm, kdim), lambda i: (i, 0)), _const_spec(w.shape)],
        out_specs=pl.BlockSpec((tm, w.shape[1] // 2), lambda i: (i, 0)),
        compiler_params=_cp(("parallel",)),
        name="glu_matmul",
    )(y, w)


def s5_mixer(u_tok, tables, w_glu, n_batch, seq, ctx_len):
    w_t, w_z, w_c, a_re, a_im = tables
    tc = SSM_CHUNK
    n = seq + ctx_len
    kc, k_lat, k_ctx = n // tc, seq // tc, ctx_len // tc
    u_seq = jnp.concatenate([u_tok[:n_batch * seq].reshape(n_batch, seq, MIX_W),
                             u_tok[n_batch * seq:].reshape(n_batch, ctx_len, MIX_W)], axis=1)
    u_ch = u_seq.reshape(n_batch, kc, tc, SSM_GROUPS, SSM_CH).transpose(0, 3, 1, 2, 4).reshape(n_batch * SSM_GROUPS, kc, tc * SSM_CH)
    z = s5_chunk_sums(u_ch, w_z)
    h = s5_state_scan(z.transpose(1, 0, 2), jnp.tile(a_re, (n_batch, 1)), jnp.tile(a_im, (n_batch, 1)), k_lat, k_ctx)
    y = s5_outputs(u_ch, h.transpose(1, 0, 2), w_t, w_c)
    y = y.reshape(n_batch, SSM_GROUPS, kc, tc, SSM_CH).transpose(0, 2, 3, 1, 4).reshape(n_batch, n, MIX_W)
    y_tok = jnp.concatenate([y[:, :seq].reshape(n_batch * seq, MIX_W), y[:, seq:].reshape(n_batch * ctx_len, MIX_W)], axis=0)
    return glu_matmul(y_tok, w_glu, 512)


def _chan_dft_kernel(x_ref, w_ref, yc_ref, ys_ref):
    x = x_ref[...]
    for g in range(FNO_GROUPS):
        y = _dot(x[:, FNO_CH * g:FNO_CH * (g + 1)], w_ref[...])
        yc_ref[:, FNO_CH * g:FNO_CH * (g + 1)] = y[:, :FNO_CH].astype(BF16)
        ys_ref[:, FNO_CH * g:FNO_CH * (g + 1)] = y[:, FNO_CH:].astype(BF16)


def chan_dft(x, w, tm):
    t, cw = x.shape
    return pl.pallas_call(
        _chan_dft_kernel,
        out_shape=[jax.ShapeDtypeStruct((t, cw), BF16)] * 2,
        grid=(t // tm,),
        in_specs=[pl.BlockSpec((tm, cw), lambda i: (i, 0)), _const_spec(w.shape)],
        out_specs=[pl.BlockSpec((tm, cw), lambda i: (i, 0))] * 2,
        compiler_params=_cp(("parallel",)),
        name="chan_dft",
    )(x, w)


def _pos_dft_kernel(c_ref, s_ref, yc_ref, ys_ref, o_ref, acc_ref, *, n_k):
    j = pl.program_id(2)

    @pl.when(j == 0)
    def _():
        acc_ref[...] = jnp.zeros_like(acc_ref)

    acc_ref[...] += _dot(c_ref[...], yc_ref[0]) - _dot(s_ref[...], ys_ref[0])

    @pl.when(j == n_k - 1)
    def _():
        o_ref[0] = acc_ref[...].astype(o_ref.dtype)


def pos_dft(cos_t, sin_t, yc, ys, tm, tk):
    nb, n, w = yc.shape
    n_k = n // tk
    return pl.pallas_call(
        functools.partial(_pos_dft_kernel, n_k=n_k),
        out_shape=jax.ShapeDtypeStruct((nb, n, w), BF16),
        grid=(nb, n // tm, n_k),
        in_specs=[pl.BlockSpec((tm, tk), lambda b, i, j: (i, j)), pl.BlockSpec((tm, tk), lambda b, i, j: (i, j)),
                  pl.BlockSpec((1, tk, w), lambda b, i, j: (b, j, 0)), pl.BlockSpec((1, tk, w), lambda b, i, j: (b, j, 0))],
        out_specs=pl.BlockSpec((1, tm, w), lambda b, i, j: (b, i, 0)),
        scratch_shapes=[pltpu.VMEM((tm, w), F32)],
        compiler_params=_cp(("parallel", "parallel", "arbitrary")),
        name="pos_dft",
    )(cos_t, sin_t, yc, ys)


def dft_tables(n):
    idx = (jnp.arange(n, dtype=jnp.int32)[:, None] * jnp.arange(n, dtype=jnp.int32)[None, :]) % n
    ang = idx.astype(F32) * (2.0 * math.pi / n)
    return jnp.cos(ang).astype(BF16), jnp.sin(ang).astype(BF16)


def chan_table(n_pos):
    c = jnp.arange(FNO_CH, dtype=jnp.int32)
    ang = ((c[:, None] * c[None, :]) % FNO_CH).astype(F32) * (2.0 * math.pi / FNO_CH)
    scale = 1.0 / math.sqrt(n_pos * FNO_CH)
    return (jnp.concatenate([jnp.cos(ang), jnp.sin(ang)], axis=1) * scale).astype(BF16)


def fourier_mixer(x_seq, pos_tabs, tm, tk):
    nb, n, w = x_seq.shape
    yc, ys = chan_dft(x_seq.reshape(nb * n, w), chan_table(n), min(512, n))
    return pos_dft(pos_tabs[0], pos_tabs[1], yc.reshape(nb, n, w), ys.reshape(nb, n, w), tm, tk)


FFT_CH = 8


def fft_tables(n):
    a = n // LANE
    ia = jnp.arange(a, dtype=jnp.int32)
    ib = jnp.arange(LANE, dtype=jnp.int32)
    ang_a = ((ia[:, None] * ia[None, :]) % a).astype(F32) * (2.0 * math.pi / a)
    ang_b = ((ib[:, None] * ib[None, :]) % LANE).astype(F32) * (2.0 * math.pi / LANE)
    ang_t = (ia[:, None] * ib[None, :]).astype(F32) * (2.0 * math.pi / n)
    f_a = jnp.concatenate([jnp.cos(ang_a), jnp.sin(ang_a)], axis=0).astype(BF16)
    return f_a, jnp.cos(ang_b).astype(BF16), jnp.sin(ang_b).astype(BF16), jnp.cos(ang_t), -jnp.sin(ang_t)


def _fft_kernel(yc_ref, ys_ref, fa_ref, cb_ref, sb_ref, tr_ref, ti_ref, o_ref, *, a):
    fa, cb, sb, tr, ti = fa_ref[...], cb_ref[...], sb_ref[...], tr_ref[...], ti_ref[...]
    for j in range(FFT_CH):
        y = _dot(fa, jnp.concatenate([yc_ref[0, j], ys_ref[0, j]], axis=1))
        gr = y[:a, :LANE] - y[a:, LANE:]
        gi = -(y[:a, LANE:] + y[a:, :LANE])
        ar = (gr * tr - gi * ti).astype(BF16)
        ai = (gr * ti + gi * tr).astype(BF16)
        o_ref[0, j] = (_dot_nt(cb, ar) + _dot_nt(sb, ai)).astype(o_ref.dtype)


def fft_positions(yc, ys, tabs):
    nb, w, a, _ = yc.shape
    xin = lambda: pl.BlockSpec((1, FFT_CH, a, LANE), lambda b, c: (b, c, 0, 0))
    return pl.pallas_call(
        functools.partial(_fft_kernel, a=a),
        out_shape=jax.ShapeDtypeStruct((nb, w, LANE, a), BF16),
        grid=(nb, w // FFT_CH),
        in_specs=[xin(), xin()] + [_const_spec(t.shape) for t in tabs],
        out_specs=pl.BlockSpec((1, FFT_CH, LANE, a), lambda b, c: (b, c, 0, 0)),
        compiler_params=_cp(("parallel", "parallel")),
        name="fft_positions",
    )(yc, ys, *tabs)


def fourier_mixer_fft(x_seq, tabs):
    nb, n, w = x_seq.shape
    yc, ys = chan_dft(x_seq.reshape(nb * n, w), chan_table(n), min(512, n))
    chan_major = lambda y: y.reshape(nb, n, w).transpose(0, 2, 1).reshape(nb, w, n // LANE, LANE)
    z = fft_positions(chan_major(yc), chan_major(ys), tabs)
    return z.reshape(nb, w, n).transpose(0, 2, 1)


def _route(logits_t, bias_col):
    score = jax.nn.sigmoid(logits_t)
    sel = score + bias_col
    sa, sb, sc, sd = (sel[8 * i:8 * (i + 1)] for i in range(EXPERTS_PER_GROUP))
    ra, rb, rc, rd = (score[8 * i:8 * (i + 1)] for i in range(EXPERTS_PER_GROUP))
    m1, n1 = jnp.maximum(sa, sb), jnp.minimum(sa, sb)
    m2, n2 = jnp.maximum(sc, sd), jnp.minimum(sc, sd)
    gsum = jnp.maximum(m1, m2) + jnp.maximum(jnp.minimum(m1, m2), jnp.maximum(n1, n2))
    rows = lax.broadcasted_iota(jnp.int32, gsum.shape, 0)
    neg = jnp.float32(-jnp.inf)
    gsum = jnp.where(rows < N_EXPERT_GROUPS, gsum, neg)
    best = jnp.max(gsum, axis=0, keepdims=True)
    grp_f = jnp.min(jnp.where(gsum == best, rows.astype(F32), 8.0), axis=0, keepdims=True)
    grp = grp_f.astype(jnp.int32)
    hot = rows == grp
    pick = lambda v: jnp.sum(jnp.where(hot, v, 0.0), axis=0, keepdims=True)
    a, b, c, d = pick(sa), pick(sb), pick(sc), pick(sd)
    wa, wb, wc, wd = pick(ra), pick(rb), pick(rc), pick(rd)

    def first_max(va, vb, vc, vd):
        m = jnp.maximum(jnp.maximum(va, vb), jnp.maximum(vc, vd))
        return jnp.where(va == m, 0, jnp.where(vb == m, 1, jnp.where(vc == m, 2, 3)))

    i1 = first_max(a, b, c, d)
    i2 = first_max(jnp.where(i1 == 0, neg, a), jnp.where(i1 == 1, neg, b), jnp.where(i1 == 2, neg, c), jnp.where(i1 == 3, neg, d))
    gate = lambda i: jnp.where(i == 0, wa, jnp.where(i == 1, wb, jnp.where(i == 2, wc, wd)))
    g1, g2 = gate(i1), gate(i2)
    den = g1 + g2
    return grp * EXPERTS_PER_GROUP + i1, grp * EXPERTS_PER_GROUP + i2, g1 / den, g2 / den


def _outproj_kernel(pa_ref, pb_ref, pc_ref, pd_ref, x_ref, w_ref, g1_ref, gam_ref, sh_ref, sc_ref, rw_ref, rb_ref,
                    xo_ref, h_ref, e_ref, gt_ref):
    acc = _dot(pa_ref[...], w_ref[0:MIX_W])
    for i, p_ref in enumerate((pb_ref, pc_ref, pd_ref), start=1):
        acc += _dot(p_ref[...], w_ref[MIX_W * i:MIX_W * (i + 1)])
    x = x_ref[...] + g1_ref[0] * acc
    xo_ref[...] = x
    h = _rms(x, gam_ref[...]) * (1.0 + sc_ref[0]) + sh_ref[0]
    h_ref[...] = h.astype(BF16)
    logits = _dot3(h, rw_ref[...])
    e1, e2, g1, g2 = _route(logits.T[:32], rb_ref[...])
    rows = lax.broadcasted_iota(jnp.int32, e_ref.shape, 0)
    e_ref[...] = jnp.where(rows == 0, e1, jnp.where(rows == 1, e2, 0))
    gt_ref[...] = jnp.where(rows == 0, g1, jnp.where(rows == 1, g2, 0.0))


def out_projection(parts, tok, w_out, gate1, gamma, shift, scale, rw, rb, n_rows, lat_tiles_per_batch, n_batch, tm):
    d = tok.shape[1]
    mod_idx = lambda i: (jnp.minimum(i // lat_tiles_per_batch, n_batch), 0, 0)
    row = lambda w: pl.BlockSpec((tm, w), lambda i: (i, 0))
    mod = lambda: pl.BlockSpec((1, 1, d), mod_idx)
    return pl.pallas_call(
        _outproj_kernel,
        out_shape=[jax.ShapeDtypeStruct((n_rows, d), F32), jax.ShapeDtypeStruct((n_rows, d), BF16),
                   jax.ShapeDtypeStruct((8, n_rows), jnp.int32), jax.ShapeDtypeStruct((8, n_rows), F32)],
        grid=(n_rows // tm,),
        in_specs=[row(MIX_W), row(MIX_W), row(MIX_W), row(MIX_W), row(d),
                  pl.BlockSpec(w_out.shape, lambda i: (0, 0), pipeline_mode=pl.Buffered(1)),
                  mod(), _const_spec((1, d)), mod(), mod(), _const_spec(rw.shape), _const_spec(rb.shape)],
        out_specs=[row(d), row(d), pl.BlockSpec((8, tm), lambda i: (0, i)), pl.BlockSpec((8, tm), lambda i: (0, i))],
        compiler_params=_cp(("parallel",)),
        name="out_projection",
    )(*parts, tok, w_out, gate1, gamma.reshape(1, d), shift, scale, rw, rb)


def _moe_kernel(be_ref, nu_ref, x_ref, wg_ref, wu_ref, wd_ref, y_ref):
    i = pl.program_id(0)

    @pl.when(i < nu_ref[0])
    def _():
        x = x_ref[...]
        a = (jax.nn.silu(_dot(x, wg_ref[0])) * _dot(x, wu_ref[0])).astype(BF16)
        y_ref[...] = _dot(a, wd_ref[0])

    @pl.when(i >= nu_ref[0])
    def _():
        y_ref[...] = jnp.zeros_like(y_ref)


def moe_experts(xb, block_expert, n_used, wg, wu, wd):
    n_rows, d = xb.shape
    n_blocks = n_rows // MOE_BLOCK
    de = wg.shape[2]
    grid_spec = pltpu.PrefetchScalarGridSpec(
        num_scalar_prefetch=2,
        grid=(n_blocks,),
        in_specs=[pl.BlockSpec((MOE_BLOCK, d), lambda i, be, nu: (i, 0)),
                  pl.BlockSpec((1, d, de), lambda i, be, nu: (be[i], 0, 0)), pl.BlockSpec((1, d, de), lambda i, be, nu: (be[i], 0, 0)),
                  pl.BlockSpec((1, de, d), lambda i, be, nu: (be[i], 0, 0))],
        out_specs=pl.BlockSpec((MOE_BLOCK, d), lambda i, be, nu: (i, 0)),
    )
    return pl.pallas_call(
        _moe_kernel,
        out_shape=jax.ShapeDtypeStruct((n_rows, d), F32),
        grid_spec=grid_spec,
        compiler_params=_cp(("arbitrary",)),
        name="moe_experts",
    )(block_expert, n_used, xb, wg, wu, wd)


def routed_moe(h, eid, gate, wg, wu, wd):
    t = h.shape[0]
    bm = MOE_BLOCK
    flat_e = eid[:2].reshape(-1)
    flat_tok = jnp.tile(jnp.arange(t, dtype=jnp.int32), 2)
    hot = (flat_e[:, None] == jnp.arange(N_EXPERTS, dtype=jnp.int32)[None, :]).astype(jnp.int32)
    csum = jnp.cumsum(hot, axis=0)
    counts = csum[-1]
    padded = (counts + bm - 1) // bm * bm
    pad_end = jnp.cumsum(padded)
    pad_start = pad_end - padded
    dest = jnp.sum(hot * (pad_start[None, :] + csum - 1), axis=1)
    n_blocks = (2 * t + N_EXPERTS * (bm - 1) + bm - 1) // bm
    n_rows = n_blocks * bm
    row_tok = jnp.zeros((n_rows,), jnp.int32).at[dest].set(flat_tok)
    block_expert = jnp.minimum(jnp.searchsorted(pad_end, jnp.arange(n_blocks, dtype=jnp.int32) * bm, side='right'),
                               N_EXPERTS - 1).astype(jnp.int32)
    n_used = (pad_end[-1] // bm).astype(jnp.int32).reshape(1)
    yb = moe_experts(h[row_tok], block_expert, n_used, wg, wu, wd)
    pos = dest.reshape(2, t)
    return yb[pos[0]], yb[pos[1]], gate[0].reshape(t, 1), gate[1].reshape(t, 1)


def _moe_residual(x_ref, y0_ref, y1_ref, w0_ref, w1_ref, g_ref):
    return x_ref[...] + g_ref[0] * (w0_ref[...] * y0_ref[...] + w1_ref[...] * y1_ref[...])


def _residual_kernel(x_ref, y0_ref, y1_ref, w0_ref, w1_ref, g_ref, o_ref):
    o_ref[...] = _moe_residual(x_ref, y0_ref, y1_ref, w0_ref, w1_ref, g_ref)


def _final_kernel(x_ref, y0_ref, y1_ref, w0_ref, w1_ref, g_ref, gam_ref, o_ref):
    o_ref[...] = _rms(_moe_residual(x_ref, y0_ref, y1_ref, w0_ref, w1_ref, g_ref), gam_ref[...])


def gated_residual(x, moe_out, gate, lat_tiles_per_batch, n_batch, tm, final_gamma=None):
    t, d = x.shape
    mod_idx = lambda i: (jnp.minimum(i // lat_tiles_per_batch, n_batch), 0, 0)
    row = lambda: pl.BlockSpec((tm, d), lambda i: (i, 0))
    col = lambda: pl.BlockSpec((tm, 1), lambda i: (i, 0))
    in_specs = [row(), row(), row(), col(), col(), pl.BlockSpec((1, 1, d), mod_idx)]
    args = [x, *moe_out, gate]
    body = _residual_kernel
    if final_gamma is not None:
        in_specs.append(_const_spec((1, d)))
        args.append(final_gamma.reshape(1, d))
        body = _final_kernel
    return pl.pallas_call(
        body,
        out_shape=jax.ShapeDtypeStruct((t, d), F32),
        grid=(t // tm,),
        in_specs=in_specs,
        out_specs=row(),
        compiler_params=_cp(("parallel",)),
        name="gated_residual",
    )(*args)


def _rope_tables(seq, ctx_len):
    pos = jnp.arange(seq)
    row = (pos // GRID_W).astype(F32)
    col = (pos % GRID_W).astype(F32)

    def tab(dim):
        quarter = dim // 4
        inv = ROPE_THETA ** (-jnp.arange(quarter, dtype=F32) / quarter)
        ang = jnp.concatenate([row[:, None] * inv, col[:, None] * inv], axis=-1)
        cos, sin = jnp.cos(ang), jnp.sin(ang)
        cos_f = jnp.concatenate([cos, cos], axis=-1)
        sin_f = jnp.concatenate([-sin, sin], axis=-1)
        cos_f = jnp.concatenate([cos_f, jnp.ones((ctx_len, dim), F32)], axis=0)
        sin_f = jnp.concatenate([sin_f, jnp.zeros((ctx_len, dim), F32)], axis=0)
        pad = LANE - dim
        return jnp.pad(cos_f, ((0, 0), (0, pad))), jnp.pad(sin_f, ((0, 0), (0, pad)))

    cm, sm = tab(MLA_ROPE)
    cg, sg = tab(GQA_HD)
    return cm, sm, cg, sg


def _half_swap(w, dim):
    return jnp.concatenate([w[..., dim // 2:], w[..., :dim // 2]], axis=-1)


def _layout_w_in(w):
    d = w.shape[0]
    k_rot = w[:, MLA_Q_RANK + MLA_KV_RANK:MLA_IN]
    z = jnp.zeros((d, LANE - MLA_ROPE), w.dtype)
    return jnp.concatenate([w[:, :MLA_Q_RANK + MLA_KV_RANK], k_rot, z, _half_swap(k_rot, MLA_ROPE), z, w[:, OFF_B:]], axis=1).astype(BF16)


def _layout_w_uq(w):
    r = w.shape[0]
    wh = w.reshape(r, MLA_HEADS, MLA_NOPE + MLA_ROPE)
    rope = wh[..., MLA_NOPE:]
    z = jnp.zeros((r, MLA_HEADS, LANE - MLA_ROPE), w.dtype)
    main = jnp.concatenate([wh[..., :MLA_NOPE], rope, z], axis=-1).reshape(r, MLA_HEADS * MLA_DQ)
    rot = jnp.concatenate([_half_swap(rope, MLA_ROPE), z], axis=-1).reshape(r, MLA_HEADS * LANE)
    return jnp.concatenate([main, rot], axis=1).astype(BF16)


def _layout_router(router_w, router_bias):
    d = router_w.shape[0]
    w4 = router_w.reshape(d, N_EXPERT_GROUPS, EXPERTS_PER_GROUP).transpose(0, 2, 1)
    w = jnp.pad(w4, ((0, 0), (0, 0), (0, 8 - N_EXPERT_GROUPS))).reshape(d, 8 * EXPERTS_PER_GROUP)
    w = jnp.pad(w, ((0, 0), (0, LANE - 8 * EXPERTS_PER_GROUP))).astype(F32)
    b4 = router_bias.astype(F32).reshape(N_EXPERT_GROUPS, EXPERTS_PER_GROUP).T
    b = jnp.pad(b4, ((0, 0), (0, 8 - N_EXPERT_GROUPS))).reshape(8 * EXPERTS_PER_GROUP, 1)
    return w, b


def kernel(x, c, ctx, c_ctx, ada_w, ada_b, norm1_g, norm2_g, w_in, mla_q_norm_g, mla_w_uq, mla_kv_norm_g, mla_w_ukv, gqa_q_norm_g, gqa_k_norm_g, ssm_lam_re, ssm_lam_im, ssm_log_dt, ssm_b_re, ssm_b_im, ssm_c_re, ssm_c_im, ssm_d, ssm_w_glu, w_out, router_w, router_bias, moe_w_gate, moe_w_up, moe_w_down, final_norm_g):
    nb, seq, d = x.shape
    ctx_len = ctx.shape[1]
    depth = ada_w.shape[0]
    n_lat, n_ctx = nb * seq, nb * ctx_len
    tm = min(512, n_ctx)
    lat_tiles = seq // tm
    rows_q = min(1024, seq)
    tk_lat = min(4096, seq)
    in_widths = (768, GQA_IN, MIX_W, MIX_W)

    tok = jnp.concatenate([x.reshape(n_lat, d), ctx.reshape(n_ctx, d)], axis=0)
    silu = jnp.concatenate([jax.nn.silu(c), jnp.broadcast_to(jax.nn.silu(c_ctx)[None], (8 - nb, d))], axis=0)
    rope_tabs = _rope_tables(seq, ctx_len)
    rw, rb = _layout_router(router_w, router_bias)
    wg_bf, wu_bf, wd_bf = cast_bf16(moe_w_gate), cast_bf16(moe_w_up), cast_bf16(moe_w_down)
    fft_lat = fft_tables(seq)
    dft_ctx = dft_tables(ctx_len)

    for l in range(depth):
        need_ctx = l < depth - 1
        mod = mod_vectors(silu, ada_w[l], ada_b[l])[:nb + 1].reshape(nb + 1, 1, 6, d)
        sh1, sc1, g1, sh2, sc2, g2 = (mod[:, :, i] for i in range(6))

        pa, pb, pc, pd = in_projection(tok, norm1_g[l], sh1, sc1, _layout_w_in(w_in[l]), in_widths, lat_tiles, nb, tm)

        qm, km, vm, qg, kg, vg = qkv_prepare(pa, pb, rope_tabs, mla_q_norm_g[l], mla_kv_norm_g[l], gqa_q_norm_g[l],
                                             gqa_k_norm_g[l], _layout_w_uq(mla_w_uq[l]), mla_w_ukv[l].astype(BF16),
                                             nb, seq, ctx_len, min(256, ctx_len))
        qm5 = qm.reshape(nb, MLA_HEADS, 1, seq + ctx_len, MLA_DQ)
        out_a = flash_attention(qm5, km, vm, seq, ctx_len, rows_q, tk_lat).reshape(n_lat, MIX_W)
        out_b = flash_attention(qg, kg, vg, seq, ctx_len, rows_q // qg.shape[2], tk_lat).reshape(n_lat, MIX_W)

        s5_tabs = s5_tables(ssm_lam_re[l], ssm_lam_im[l], ssm_log_dt[l], ssm_b_re[l], ssm_b_im[l], ssm_c_re[l], ssm_c_im[l], ssm_d[l])
        out_c = s5_mixer(pc, s5_tabs, ssm_w_glu[l].astype(BF16), nb, seq, ctx_len)

        out_d = fourier_mixer_fft(pd[:n_lat].reshape(nb, seq, MIX_W), fft_lat).reshape(n_lat, MIX_W)

        if need_ctx:
            ctx_a = context_attention(qm5, km, vm, seq, ctx_len).reshape(n_ctx, MIX_W)
            ctx_b = context_attention(qg, kg, vg, seq, ctx_len).reshape(n_ctx, MIX_W)
            ctx_d = fourier_mixer(pd[n_lat:].reshape(nb, ctx_len, MIX_W), dft_ctx, ctx_len, ctx_len).reshape(n_ctx, MIX_W)
            parts = [jnp.concatenate([out_a, ctx_a], axis=0), jnp.concatenate([out_b, ctx_b], axis=0), out_c,
                     jnp.concatenate([out_d, ctx_d], axis=0)]
            n_rows = n_lat + n_ctx
        else:
            parts = [out_a, out_b, out_c, out_d]
            n_rows = n_lat

        tok, h2, eid, gate = out_projection(parts, tok, w_out[l].astype(BF16), g1, norm2_g[l], sh2, sc2, rw, rb,
                                            n_rows, 2 * lat_tiles, nb, tm // 2)
        y = routed_moe(h2, eid, gate, wg_bf[l], wu_bf[l], wd_bf[l])
        tok = gated_residual(tok, y, g2, lat_tiles, nb, tm, None if need_ctx else final_norm_g)

    return tok[:n_lat].reshape(nb, seq, d)
```

```python
import functools
import math

import jax
import jax.numpy as jnp
from jax import lax
from jax.experimental import pallas as pl
from jax.experimental.pallas import tpu as pltpu

F32 = jnp.float32
BF16 = jnp.bfloat16

D_MODEL = 2048
GRID_W = 64
ROPE_THETA = 10000.0
RMS_EPS = 1e-6
MIX_W = D_MODEL // 4
MLA_HEADS = 4
MLA_NOPE = 128
MLA_V = 128
MLA_ROPE = 64
MLA_Q_RANK = 384
MLA_KV_RANK = 128
MLA_IN = MLA_Q_RANK + MLA_KV_RANK + MLA_ROPE
MLA_DQ = 256
GQA_HEADS = 4
GQA_KV_HEADS = 2
GQA_HD = 128
GQA_IN = (GQA_HEADS + 2 * GQA_KV_HEADS) * GQA_HD
SSM_GROUPS = 32
SSM_CH = 16
SSM_STATE = 64
SSM_CHUNK = 16
FNO_GROUPS = 4
FNO_CH = 128
OFF_B = MLA_IN
OFF_C = OFF_B + GQA_IN
OFF_D = OFF_C + MIX_W
N_EXPERTS = 16
N_EXPERT_GROUPS = 4
EXPERTS_PER_GROUP = 4
D_EXPERT = D_MODEL // 2
MOE_BLOCK = 256
LANE = 128
MXU_W = 256
LOG2E = 1.4426950408889634
VMEM_LIMIT = 52 * 1024 * 1024


def _cp(sem, vmem=VMEM_LIMIT):
    return pltpu.CompilerParams(dimension_semantics=sem, vmem_limit_bytes=vmem)


def _const_spec(shape):
    nd = len(shape)
    return pl.BlockSpec(shape, lambda *_: (0,) * nd)


def _split_bf16(a):
    hi = a.astype(BF16)
    lo = (a - hi.astype(F32)).astype(BF16)
    return hi, lo


def _dot(a, b):
    return jnp.dot(a, b, preferred_element_type=F32)


def _dot3(a, b):
    ah, al = _split_bf16(a)
    bh, bl = _split_bf16(b)
    return _dot(ah, bh) + _dot(al, bh) + _dot(ah, bl)


def _rms(x, g):
    return x * lax.rsqrt(jnp.mean(x * x, axis=-1, keepdims=True) + RMS_EPS) * g


def _cast_kernel(x_ref, o_ref):
    o_ref[...] = x_ref[...].astype(o_ref.dtype)


def cast_bf16(w, rows_per_block=1024):
    c = w.shape[-1]
    r = w.size // c
    out = pl.pallas_call(
        _cast_kernel,
        out_shape=jax.ShapeDtypeStruct((r, c), BF16),
        grid=(r // rows_per_block,),
        in_specs=[pl.BlockSpec((rows_per_block, c), lambda i: (i, 0))],
        out_specs=pl.BlockSpec((rows_per_block, c), lambda i: (i, 0)),
        compiler_params=_cp(("parallel",)),
        name="cast_bf16",
    )(w.reshape(r, c))
    return out.reshape(w.shape)


def _mod_kernel(s_ref, w_ref, b_ref, o_ref):
    o_ref[...] = _dot3(s_ref[...], w_ref[...]) + b_ref[...]


def mod_vectors(s8, w, b):
    k, n = w.shape
    tn = 1024
    return pl.pallas_call(
        _mod_kernel,
        out_shape=jax.ShapeDtypeStruct((8, n), F32),
        grid=(n // tn,),
        in_specs=[_const_spec((8, k)), pl.BlockSpec((k, tn), lambda j: (0, j)), pl.BlockSpec((1, tn), lambda j: (0, j))],
        out_specs=pl.BlockSpec((8, tn), lambda j: (0, j)),
        compiler_params=_cp(("parallel",)),
        name="mod_vectors",
    )(s8, w, b.reshape(1, n))


def _inproj_kernel(x_ref, g_ref, sh_ref, sc_ref, w_ref, *o_refs, widths):
    h = _rms(x_ref[...], g_ref[...])
    hb = (h * (1.0 + sc_ref[0]) + sh_ref[0]).astype(BF16)
    off = 0
    for o_ref, wd in zip(o_refs, widths):
        o_ref[...] = _dot(hb, w_ref[:, off:off + wd]).astype(o_ref.dtype)
        off += wd


def in_projection(tok, gamma, shift, scale, w, widths, lat_tiles_per_batch, n_batch, tm):
    t, d = tok.shape
    mod_idx = lambda i: (jnp.minimum(i // lat_tiles_per_batch, n_batch), 0, 0)
    return pl.pallas_call(
        functools.partial(_inproj_kernel, widths=widths),
        out_shape=[jax.ShapeDtypeStruct((t, wd), BF16) for wd in widths],
        grid=(t // tm,),
        in_specs=[pl.BlockSpec((tm, d), lambda i: (i, 0)), _const_spec((1, d)),
                  pl.BlockSpec((1, 1, d), mod_idx), pl.BlockSpec((1, 1, d), mod_idx),
                  pl.BlockSpec(w.shape, lambda i: (0, 0), pipeline_mode=pl.Buffered(1))],
        out_specs=[pl.BlockSpec((tm, wd), lambda i: (i, 0)) for wd in widths],
        compiler_params=_cp(("parallel",)),
        name="in_projection",
    )(tok, gamma.reshape(1, d), shift, scale, w)


def _qkv_kernel(a_ref, b_ref, cm_ref, sm_ref, cg_ref, sg_ref, gq_ref, gkv_ref, ggq_ref, ggk_ref, wuq_ref, wukv_ref,
                qm_ref, km_ref, vm_ref, qg_ref, kg_ref, vg_ref):
    a = a_ref[...].astype(F32)
    cm, sm = cm_ref[...], sm_ref[...]
    mla_scale = LOG2E * (MLA_NOPE + MLA_ROPE) ** -0.5
    qn = _rms(a[:, :MLA_Q_RANK], gq_ref[...]).astype(BF16)
    q_all = _dot(qn, wuq_ref[...])
    rot0 = MLA_HEADS * MLA_DQ
    for h in range(MLA_HEADS):
        nope = q_all[:, MLA_DQ * h:MLA_DQ * h + LANE]
        rp = q_all[:, MLA_DQ * h + LANE:MLA_DQ * (h + 1)]
        rr = q_all[:, rot0 + LANE * h:rot0 + LANE * (h + 1)]
        qm_ref[0, h, :, :LANE] = (nope * mla_scale).astype(BF16)
        qm_ref[0, h, :, LANE:] = ((rp * cm + rr * sm) * mla_scale).astype(BF16)
    kvn = _rms(a[:, MLA_Q_RANK:MLA_Q_RANK + MLA_KV_RANK], gkv_ref[...]).astype(BF16)
    kv = _dot(kvn, wukv_ref[...])
    kr = (a[:, 512:640] * cm + a[:, 640:768] * sm).astype(BF16)
    for h in range(MLA_HEADS):
        km_ref[0, h, :, :LANE] = kv[:, 256 * h:256 * h + LANE].astype(BF16)
        km_ref[0, h, :, LANE:] = kr
        vm_ref[0, h] = kv[:, 256 * h + LANE:256 * (h + 1)].T.astype(BF16)
    b = b_ref[...].astype(F32)
    cg, sg = cg_ref[...], sg_ref[...]
    gqa_scale = LOG2E * GQA_HD ** -0.5
    grp = GQA_HEADS // GQA_KV_HEADS
    for h in range(GQA_HEADS):
        qh = _rms(b[:, GQA_HD * h:GQA_HD * (h + 1)], ggq_ref[...])
        qh = qh * cg + pltpu.roll(qh, GQA_HD // 2, 1) * sg
        qg_ref[0, h // grp, h % grp] = (qh * gqa_scale).astype(BF16)
    k0 = GQA_HEADS * GQA_HD
    v0 = k0 + GQA_KV_HEADS * GQA_HD
    for h in range(GQA_KV_HEADS):
        kh = _rms(b[:, k0 + GQA_HD * h:k0 + GQA_HD * (h + 1)], ggk_ref[...])
        kg_ref[0, h] = (kh * cg + pltpu.roll(kh, GQA_HD // 2, 1) * sg).astype(BF16)
        vg_ref[0, h] = b[:, v0 + GQA_HD * h:v0 + GQA_HD * (h + 1)].T.astype(BF16)

def qkv_prepare(pa, pb, tabs, g_q, g_kv, g_gq, g_gk, w_uq, w_ukv, n_batch, seq, ctx_len, tm):
    t = pa.shape[0]
    n = seq + ctx_len
    lat_tiles = seq // tm
    ctx_tiles = ctx_len // tm
    n_lat = n_batch * lat_tiles

    def bidx(i):
        j = i - n_lat
        bi = jnp.where(i < n_lat, i // lat_tiles, j // ctx_tiles)
        blk = jnp.where(i < n_lat, i % lat_tiles, lat_tiles + j % ctx_tiles)
        return bi, blk

    def o4(i):
        bi, blk = bidx(i)
        return (bi, 0, blk, 0)

    def o4t(i):
        bi, blk = bidx(i)
        return (bi, 0, 0, blk)

    def o5(i):
        bi, blk = bidx(i)
        return (bi, 0, 0, blk, 0)

    tab_spec = pl.BlockSpec((tm, LANE), lambda i: (bidx(i)[1], 0))
    vec = lambda w: _const_spec((1, w))
    return pl.pallas_call(
        _qkv_kernel,
        out_shape=[jax.ShapeDtypeStruct((n_batch, MLA_HEADS, n, MLA_DQ), BF16),
                   jax.ShapeDtypeStruct((n_batch, MLA_HEADS, n, MLA_DQ), BF16),
                   jax.ShapeDtypeStruct((n_batch, MLA_HEADS, MLA_V, n), BF16),
                   jax.ShapeDtypeStruct((n_batch, GQA_KV_HEADS, GQA_HEADS // GQA_KV_HEADS, n, GQA_HD), BF16),
                   jax.ShapeDtypeStruct((n_batch, GQA_KV_HEADS, n, GQA_HD), BF16),
                   jax.ShapeDtypeStruct((n_batch, GQA_KV_HEADS, GQA_HD, n), BF16)],
        grid=(t // tm,),
        in_specs=[pl.BlockSpec((tm, pa.shape[1]), lambda i: (i, 0)), pl.BlockSpec((tm, pb.shape[1]), lambda i: (i, 0)),
                  tab_spec, tab_spec, tab_spec, tab_spec,
                  vec(MLA_Q_RANK), vec(MLA_KV_RANK), vec(GQA_HD), vec(GQA_HD),
                  _const_spec(w_uq.shape), _const_spec(w_ukv.shape)],
        out_specs=[pl.BlockSpec((1, MLA_HEADS, tm, MLA_DQ), o4), pl.BlockSpec((1, MLA_HEADS, tm, MLA_DQ), o4),
                   pl.BlockSpec((1, MLA_HEADS, MLA_V, tm), o4t),
                   pl.BlockSpec((1, GQA_KV_HEADS, GQA_HEADS // GQA_KV_HEADS, tm, GQA_HD), o5),
                   pl.BlockSpec((1, GQA_KV_HEADS, tm, GQA_HD), o4), pl.BlockSpec((1, GQA_KV_HEADS, GQA_HD, tm), o4t)],
        compiler_params=_cp(("parallel",)),
        name="qkv_prepare",
    )(pa, pb, *tabs, g_q.reshape(1, -1), g_kv.reshape(1, -1), g_gq.reshape(1, -1), g_gk.reshape(1, -1), w_uq, w_ukv)


def _dot_nt(a, b):
    return lax.dot_general(a, b, (((1,), (1,)), ((), ())), preferred_element_type=F32)


def _softmax_cols(s):
    m = jnp.max(s, axis=0, keepdims=True)
    p = jnp.exp2(s - m)
    return m, p, jnp.sum(p, axis=0, keepdims=True)


def _store_heads(o_ref, o_t, grp, tq, dv):
    o = o_t.T
    for g in range(grp):
        o_ref[0, :, dv * g:dv * (g + 1)] = o[tq * g:tq * (g + 1)].astype(o_ref.dtype)


def _flash_kernel(q_ref, k_ref, vt_ref, kc_ref, vtc_ref, o_ref, m_sc, l_sc, acc_sc, *, grp, tq, dv, n_kv):
    j = pl.program_id(3)
    rows = grp * tq
    q = q_ref[0, 0].reshape(rows, q_ref.shape[-1])

    @pl.when(j == 0)
    def _():
        m, p, l = _softmax_cols(_dot_nt(kc_ref[0, 0], q))
        m_sc[...] = m
        l_sc[...] = l
        acc_sc[...] = _dot(vtc_ref[0, 0], p.astype(BF16))

    k = k_ref[0, 0]
    vt = vt_ref[0, 0]
    slabs = [slice(MXU_W * r, MXU_W * (r + 1)) for r in range(rows // MXU_W)]
    scores = [_dot_nt(k, q[sl]) for sl in slabs]
    for sl, s in zip(slabs, scores):
        m_prev = m_sc[:, sl]
        m_new = jnp.maximum(m_prev, jnp.max(s, axis=0, keepdims=True))
        alpha = jnp.exp2(m_prev - m_new)
        p = jnp.exp2(s - m_new)
        l_sc[:, sl] = alpha * l_sc[:, sl] + jnp.sum(p, axis=0, keepdims=True)
        acc_sc[:, sl] = alpha * acc_sc[:, sl] + _dot(vt, p.astype(BF16))
        m_sc[:, sl] = m_new

    @pl.when(j == n_kv - 1)
    def _():
        _store_heads(o_ref, acc_sc[...] / l_sc[...], grp, tq, dv)


def flash_attention(q, k, vt, seq, ctx_len, tq, tk):
    nb, hk, grp, n, dq = q.shape
    dv = vt.shape[-2]
    n_kv = seq // tk
    cb = seq // ctx_len
    rows = grp * tq
    return pl.pallas_call(
        functools.partial(_flash_kernel, grp=grp, tq=tq, dv=dv, n_kv=n_kv),
        out_shape=jax.ShapeDtypeStruct((nb, seq, hk * grp * dv), BF16),
        grid=(nb, hk, seq // tq, n_kv),
        in_specs=[pl.BlockSpec((1, 1, grp, tq, dq), lambda b, h, i, j: (b, h, 0, i, 0)),
                  pl.BlockSpec((1, 1, tk, dq), lambda b, h, i, j: (b, h, j, 0)),
                  pl.BlockSpec((1, 1, dv, tk), lambda b, h, i, j: (b, h, 0, j)),
                  pl.BlockSpec((1, 1, ctx_len, dq), lambda b, h, i, j: (b, h, cb, 0)),
                  pl.BlockSpec((1, 1, dv, ctx_len), lambda b, h, i, j: (b, h, 0, cb))],
        out_specs=pl.BlockSpec((1, tq, grp * dv), lambda b, h, i, j: (b, i, h)),
        scratch_shapes=[pltpu.VMEM((1, rows), F32), pltpu.VMEM((1, rows), F32), pltpu.VMEM((dv, rows), F32)],
        compiler_params=_cp(("parallel", "parallel", "parallel", "arbitrary")),
        name="flash_attention",
    )(q, k, vt, k, vt)


def _ctx_attn_kernel(q_ref, kc_ref, vtc_ref, o_ref, *, grp, tq, dv):
    q = q_ref[0, 0].reshape(grp * tq, q_ref.shape[-1])
    _, p, l = _softmax_cols(_dot_nt(kc_ref[0, 0], q))
    _store_heads(o_ref, _dot(vtc_ref[0, 0], p.astype(BF16)) / l, grp, tq, dv)


def context_attention(q, k, vt, seq, ctx_len):
    nb, hk, grp, n, dq = q.shape
    dv = vt.shape[-2]
    cb = seq // ctx_len
    return pl.pallas_call(
        functools.partial(_ctx_attn_kernel, grp=grp, tq=ctx_len, dv=dv),
        out_shape=jax.ShapeDtypeStruct((nb, ctx_len, hk * grp * dv), BF16),
        grid=(nb, hk),
        in_specs=[pl.BlockSpec((1, 1, grp, ctx_len, dq), lambda b, h: (b, h, 0, cb, 0)),
                  pl.BlockSpec((1, 1, ctx_len, dq), lambda b, h: (b, h, cb, 0)),
                  pl.BlockSpec((1, 1, dv, ctx_len), lambda b, h: (b, h, 0, cb))],
        out_specs=pl.BlockSpec((1, ctx_len, grp * dv), lambda b, h: (b, 0, h)),
        compiler_params=_cp(("parallel", "parallel")),
        name="context_attention",
    )(q, k, vt)


def s5_tables(lam_re, lam_im, log_dt, b_re, b_im, c_re, c_im, d):
    hp = lax.Precision.HIGHEST
    tc = SSM_CHUNK
    lam = lax.complex(lam_re.astype(F32), lam_im.astype(F32))
    dt = jnp.exp(log_dt.astype(F32))[..., None]
    lam_bar = jnp.exp(lam * dt)
    b_bar = ((lam_bar - 1.0) / lam)[..., None] * lax.complex(b_re.astype(F32), b_im.astype(F32))
    c_mat = lax.complex(c_re.astype(F32), c_im.astype(F32))
    steps = jnp.arange(tc + 1, dtype=F32)
    lam_pow = jnp.exp((lam * dt)[:, :, None, :] * steps[None, None, :, None])
    kern = jnp.einsum('dgip,dgtp,dgpj->dgtij', c_mat, lam_pow[:, :, :tc], b_bar, precision=hp).real
    s_idx = jnp.arange(tc)[:, None]
    t_idx = jnp.arange(tc)[None, :]
    kf = kern[0][:, jnp.clip(t_idx - s_idx, 0, tc - 1)] * (t_idx >= s_idx)[None, :, :, None, None]
    kb = kern[1][:, jnp.clip(s_idx - t_idx, 0, tc - 1)] * (s_idx >= t_idx)[None, :, :, None, None]
    dg = d.astype(F32).reshape(SSM_GROUPS, SSM_CH)
    skip = (s_idx == t_idx)[None, :, :, None, None] * (jnp.eye(SSM_CH, dtype=F32) * dg[:, :, None])[:, None, None]
    w_t = (kf + kb + skip).transpose(0, 1, 4, 2, 3).reshape(SSM_GROUPS, tc * SSM_CH, tc * SSM_CH)
    zf = lam_pow[0][:, tc - 1 - jnp.arange(tc)][..., None] * b_bar[0][:, None]
    zb = lam_pow[1][:, jnp.arange(tc)][..., None] * b_bar[1][:, None]
    to_rows = lambda z: z.transpose(0, 1, 3, 2).reshape(SSM_GROUPS, tc * SSM_CH, SSM_STATE)
    w_z = jnp.concatenate([to_rows(zf.real), to_rows(zb.real), to_rows(zf.imag), to_rows(zb.imag)], axis=-1)
    mf = c_mat[0][:, None] * lam_pow[0][:, 1 + jnp.arange(tc)][:, :, None, :]
    mb = c_mat[1][:, None] * lam_pow[1][:, tc - jnp.arange(tc)][:, :, None, :]
    to_cols = lambda m: m.transpose(0, 3, 1, 2).reshape(SSM_GROUPS, SSM_STATE, tc * SSM_CH)
    w_c = jnp.concatenate([to_cols(mf.real), to_cols(mb.real), -to_cols(mf.imag), -to_cols(mb.imag)], axis=1)
    a16 = lam_pow[:, :, tc]
    a_re = jnp.concatenate([a16[0].real, a16[1].real], axis=-1)
    a_im = jnp.concatenate([a16[0].imag, a16[1].imag], axis=-1)
    return w_t.astype(BF16), w_z.astype(BF16), w_c.astype(BF16), a_re, a_im


def _s5_z_kernel(u_ref, w_ref, z_ref):
    z_ref[0] = _dot(u_ref[0], w_ref[0])


def s5_chunk_sums(u, w_z):
    bg, kc, cw = u.shape
    return pl.pallas_call(
        _s5_z_kernel,
        out_shape=jax.ShapeDtypeStruct((bg, kc, cw), F32),
        grid=(bg,),
        in_specs=[pl.BlockSpec((1, kc, cw), lambda i: (i, 0, 0)), pl.BlockSpec((1, cw, cw), lambda i: (i % SSM_GROUPS, 0, 0))],
        out_specs=pl.BlockSpec((1, kc, cw), lambda i: (i, 0, 0)),
        compiler_params=_cp(("parallel",)),
        name="s5_chunk_sums",
    )(u, w_z)


def _s5_scan_kernel(z_ref, ar_ref, ai_ref, h_ref, *, k_lat, k_ctx):
    ar, ai = ar_ref[...], ai_ref[...]
    rows = ar.shape[0]
    fwd_lanes = lax.broadcasted_iota(jnp.int32, (rows, LANE), 1) < SSM_STATE

    def step(cf, cb, hr, hi):
        h_ref[cf, :, 0:SSM_STATE] = hr[:, :SSM_STATE]
        h_ref[cb, :, SSM_STATE:LANE] = hr[:, SSM_STATE:]
        h_ref[cf, :, LANE:LANE + SSM_STATE] = hi[:, :SSM_STATE]
        h_ref[cb, :, LANE + SSM_STATE:] = hi[:, SSM_STATE:]
        zf, zb = z_ref[cf], z_ref[cb]
        zr = jnp.where(fwd_lanes, zf[:, :LANE], zb[:, :LANE])
        zi = jnp.where(fwd_lanes, zf[:, LANE:], zb[:, LANE:])
        return ar * hr - ai * hi + zr, ar * hi + ai * hr + zi

    def ctx_body(n, c):
        return step(k_lat + n, k_lat + k_ctx - 1 - n, *c)

    def lat_body(n, c):
        return step(n, k_lat - 1 - n, *c)

    zero = jnp.zeros((rows, LANE), F32)
    carry = lax.fori_loop(0, k_ctx, ctx_body, (zero, zero))
    lax.fori_loop(0, k_lat, lat_body, carry)


def s5_state_scan(z, a_re, a_im, k_lat, k_ctx):
    kc, r, cw = z.shape
    tr = 8
    return pl.pallas_call(
        functools.partial(_s5_scan_kernel, k_lat=k_lat, k_ctx=k_ctx),
        out_shape=jax.ShapeDtypeStruct((kc, r, cw), F32),
        grid=(r // tr,),
        in_specs=[pl.BlockSpec((kc, tr, cw), lambda i: (0, i, 0)), pl.BlockSpec((tr, LANE), lambda i: (i, 0)),
                  pl.BlockSpec((tr, LANE), lambda i: (i, 0))],
        out_specs=pl.BlockSpec((kc, tr, cw), lambda i: (0, i, 0)),
        compiler_params=_cp(("parallel",)),
        name="s5_state_scan",
    )(z, a_re, a_im)


def _gelu_tanh(y):
    return 0.5 * y * (1.0 + jnp.tanh(math.sqrt(2.0 / math.pi) * (y + 0.044715 * (y * y * y))))


def _s5_y_kernel(u_ref, h_ref, wt_ref, wc_ref, y_ref):
    y = _dot(u_ref[0], wt_ref[0]) + _dot(h_ref[0].astype(BF16), wc_ref[0])
    y_ref[0] = _gelu_tanh(y).astype(y_ref.dtype)


def s5_outputs(u, h, w_t, w_c):
    bg, kc, cw = u.shape
    blk = lambda: pl.BlockSpec((1, kc, cw), lambda i: (i, 0, 0))
    wsp = lambda: pl.BlockSpec((1, cw, cw), lambda i: (i % SSM_GROUPS, 0, 0))
    return pl.pallas_call(
        _s5_y_kernel,
        out_shape=jax.ShapeDtypeStruct((bg, kc, cw), BF16),
        grid=(bg,),
        in_specs=[blk(), blk(), wsp(), wsp()],
        out_specs=blk(),
        compiler_params=_cp(("parallel",)),
        name="s5_outputs",
    )(u, h, w_t, w_c)


def _glu_kernel(y_ref, w_ref, o_ref):
    z = _dot(y_ref[...], w_ref[...])
    half = z.shape[1] // 2
    o_ref[...] = (z[:, :half] * jax.nn.sigmoid(z[:, half:])).astype(o_ref.dtype)


def glu_matmul(y, w, tm):
    t, kdim = y.shape
    return pl.pallas_call(
        _glu_kernel,
        out_shape=jax.ShapeDtypeStruct((t, w.shape[1] // 2), BF16),
        grid=(t // tm,),
        in_specs=[pl.BlockSpec((tm, kdim), lambda i: (i, 0)), _const_spec(w.shape)],
        out_specs=pl.BlockSpec((tm, w.shape[1] // 2), lambda i: (i, 0)),
        compiler_params=_cp(("parallel",)),
        name="glu_matmul",
    )(y, w)


def s5_mixer(u_tok, tables, w_glu, n_batch, seq, ctx_len):
    w_t, w_z, w_c, a_re, a_im = tables
    tc = SSM_CHUNK
    n = seq + ctx_len
    kc, k_lat, k_ctx = n // tc, seq // tc, ctx_len // tc
    u_seq = jnp.concatenate([u_tok[:n_batch * seq].reshape(n_batch, seq, MIX_W),
                             u_tok[n_batch * seq:].reshape(n_batch, ctx_len, MIX_W)], axis=1)
    u_ch = u_seq.reshape(n_batch, kc, tc, SSM_GROUPS, SSM_CH).transpose(0, 3, 1, 2, 4).reshape(n_batch * SSM_GROUPS, kc, tc * SSM_CH)
    z = s5_chunk_sums(u_ch, w_z)
    h = s5_state_scan(z.transpose(1, 0, 2), jnp.tile(a_re, (n_batch, 1)), jnp.tile(a_im, (n_batch, 1)), k_lat, k_ctx)
    y = s5_outputs(u_ch, h.transpose(1, 0, 2), w_t, w_c)
    y = y.reshape(n_batch, SSM_GROUPS, kc, tc, SSM_CH).transpose(0, 2, 3, 1, 4).reshape(n_batch, n, MIX_W)
    y_tok = jnp.concatenate([y[:, :seq].reshape(n_batch * seq, MIX_W), y[:, seq:].reshape(n_batch * ctx_len, MIX_W)], axis=0)
    return glu_matmul(y_tok, w_glu, 512)


def _chan_dft_kernel(x_ref, w_ref, yc_ref, ys_ref):
    x = x_ref[...]
    for g in range(FNO_GROUPS):
        y = _dot(x[:, FNO_CH * g:FNO_CH * (g + 1)], w_ref[...])
        yc_ref[:, FNO_CH * g:FNO_CH * (g + 1)] = y[:, :FNO_CH].astype(BF16)
        ys_ref[:, FNO_CH * g:FNO_CH * (g + 1)] = y[:, FNO_CH:].astype(BF16)


def chan_dft(x, w, tm):
    t, cw = x.shape
    return pl.pallas_call(
        _chan_dft_kernel,
        out_shape=[jax.ShapeDtypeStruct((t, cw), BF16)] * 2,
        grid=(t // tm,),
        in_specs=[pl.BlockSpec((tm, cw), lambda i: (i, 0)), _const_spec(w.shape)],
        out_specs=[pl.BlockSpec((tm, cw), lambda i: (i, 0))] * 2,
        compiler_params=_cp(("parallel",)),
        name="chan_dft",
    )(x, w)


def _pos_dft_kernel(c_ref, s_ref, yc_ref, ys_ref, o_ref, acc_ref, *, n_k):
    j = pl.program_id(2)

    @pl.when(j == 0)
    def _():
        acc_ref[...] = jnp.zeros_like(acc_ref)

    acc_ref[...] += _dot(c_ref[...], yc_ref[0]) - _dot(s_ref[...], ys_ref[0])

    @pl.when(j == n_k - 1)
    def _():
        o_ref[0] = acc_ref[...].astype(o_ref.dtype)


def pos_dft(cos_t, sin_t, yc, ys, tm, tk):
    nb, n, w = yc.shape
    n_k = n // tk
    return pl.pallas_call(
        functools.partial(_pos_dft_kernel, n_k=n_k),
        out_shape=jax.ShapeDtypeStruct((nb, n, w), BF16),
        grid=(nb, n // tm, n_k),
        in_specs=[pl.BlockSpec((tm, tk), lambda b, i, j: (i, j)), pl.BlockSpec((tm, tk), lambda b, i, j: (i, j)),
                  pl.BlockSpec((1, tk, w), lambda b, i, j: (b, j, 0)), pl.BlockSpec((1, tk, w), lambda b, i, j: (b, j, 0))],
        out_specs=pl.BlockSpec((1, tm, w), lambda b, i, j: (b, i, 0)),
        scratch_shapes=[pltpu.VMEM((tm, w), F32)],
        compiler_params=_cp(("parallel", "parallel", "arbitrary")),
        name="pos_dft",
    )(cos_t, sin_t, yc, ys)


def dft_tables(n):
    idx = (jnp.arange(n, dtype=jnp.int32)[:, None] * jnp.arange(n, dtype=jnp.int32)[None, :]) % n
    ang = idx.astype(F32) * (2.0 * math.pi / n)
    return jnp.cos(ang).astype(BF16), jnp.sin(ang).astype(BF16)


def chan_table(n_pos):
    c = jnp.arange(FNO_CH, dtype=jnp.int32)
    ang = ((c[:, None] * c[None, :]) % FNO_CH).astype(F32) * (2.0 * math.pi / FNO_CH)
    scale = 1.0 / math.sqrt(n_pos * FNO_CH)
    return (jnp.concatenate([jnp.cos(ang), jnp.sin(ang)], axis=1) * scale).astype(BF16)


def fourier_mixer(x_seq, pos_tabs, tm, tk):
    nb, n, w = x_seq.shape
    yc, ys = chan_dft(x_seq.reshape(nb * n, w), chan_table(n), min(512, n))
    return pos_dft(pos_tabs[0], pos_tabs[1], yc.reshape(nb, n, w), ys.reshape(nb, n, w), tm, tk)


FFT_CH = 8


def fft_tables(n):
    a = n // LANE
    ia = jnp.arange(a, dtype=jnp.int32)
    ib = jnp.arange(LANE, dtype=jnp.int32)
    ang_a = ((ia[:, None] * ia[None, :]) % a).astype(F32) * (2.0 * math.pi / a)
    ang_b = ((ib[:, None] * ib[None, :]) % LANE).astype(F32) * (2.0 * math.pi / LANE)
    ang_t = (ia[:, None] * ib[None, :]).astype(F32) * (2.0 * math.pi / n)
    f_a = jnp.concatenate([jnp.cos(ang_a), jnp.sin(ang_a)], axis=0).astype(BF16)
    return f_a, jnp.cos(ang_b).astype(BF16), jnp.sin(ang_b).astype(BF16), jnp.cos(ang_t), -jnp.sin(ang_t)


def _fft_kernel(yc_ref, ys_ref, fa_ref, cb_ref, sb_ref, tr_ref, ti_ref, o_ref, *, a):
    fa, cb, sb, tr, ti = fa_ref[...], cb_ref[...], sb_ref[...], tr_ref[...], ti_ref[...]
    for j in range(FFT_CH):
        y = _dot(fa, jnp.concatenate([yc_ref[0, j], ys_ref[0, j]], axis=1))
        gr = y[:a, :LANE] - y[a:, LANE:]
        gi = -(y[:a, LANE:] + y[a:, :LANE])
        ar = (gr * tr - gi * ti).astype(BF16)
        ai = (gr * ti + gi * tr).astype(BF16)
        o_ref[0, j] = (_dot_nt(cb, ar) + _dot_nt(sb, ai)).astype(o_ref.dtype)


def fft_positions(yc, ys, tabs):
    nb, w, a, _ = yc.shape
    xin = lambda: pl.BlockSpec((1, FFT_CH, a, LANE), lambda b, c: (b, c, 0, 0))
    return pl.pallas_call(
        functools.partial(_fft_kernel, a=a),
        out_shape=jax.ShapeDtypeStruct((nb, w, LANE, a), BF16),
        grid=(nb, w // FFT_CH),
        in_specs=[xin(), xin()] + [_const_spec(t.shape) for t in tabs],
        out_specs=pl.BlockSpec((1, FFT_CH, LANE, a), lambda b, c: (b, c, 0, 0)),
        compiler_params=_cp(("parallel", "parallel")),
        name="fft_positions",
    )(yc, ys, *tabs)


def fourier_mixer_fft(x_seq, tabs):
    nb, n, w = x_seq.shape
    yc, ys = chan_dft(x_seq.reshape(nb * n, w), chan_table(n), min(512, n))
    chan_major = lambda y: y.reshape(nb, n, w).transpose(0, 2, 1).reshape(nb, w, n // LANE, LANE)
    z = fft_positions(chan_major(yc), chan_major(ys), tabs)
    return z.reshape(nb, w, n).transpose(0, 2, 1)


def _route(logits_t, bias_col):
    score = jax.nn.sigmoid(logits_t)
    sel = score + bias_col
    sa, sb, sc, sd = (sel[8 * i:8 * (i + 1)] for i in range(EXPERTS_PER_GROUP))
    ra, rb, rc, rd = (score[8 * i:8 * (i + 1)] for i in range(EXPERTS_PER_GROUP))
    m1, n1 = jnp.maximum(sa, sb), jnp.minimum(sa, sb)
    m2, n2 = jnp.maximum(sc, sd), jnp.minimum(sc, sd)
    gsum = jnp.maximum(m1, m2) + jnp.maximum(jnp.minimum(m1, m2), jnp.maximum(n1, n2))
    rows = lax.broadcasted_iota(jnp.int32, gsum.shape, 0)
    neg = jnp.float32(-jnp.inf)
    gsum = jnp.where(rows < N_EXPERT_GROUPS, gsum, neg)
    best = jnp.max(gsum, axis=0, keepdims=True)
    grp_f = jnp.min(jnp.where(gsum == best, rows.astype(F32), 8.0), axis=0, keepdims=True)
    grp = grp_f.astype(jnp.int32)
    hot = rows == grp
    pick = lambda v: jnp.sum(jnp.where(hot, v, 0.0), axis=0, keepdims=True)
    a, b, c, d = pick(sa), pick(sb), pick(sc), pick(sd)
    wa, wb, wc, wd = pick(ra), pick(rb), pick(rc), pick(rd)

    def first_max(va, vb, vc, vd):
        m = jnp.maximum(jnp.maximum(va, vb), jnp.maximum(vc, vd))
        return jnp.where(va == m, 0, jnp.where(vb == m, 1, jnp.where(vc == m, 2, 3)))

    i1 = first_max(a, b, c, d)
    i2 = first_max(jnp.where(i1 == 0, neg, a), jnp.where(i1 == 1, neg, b), jnp.where(i1 == 2, neg, c), jnp.where(i1 == 3, neg, d))
    gate = lambda i: jnp.where(i == 0, wa, jnp.where(i == 1, wb, jnp.where(i == 2, wc, wd)))
    g1, g2 = gate(i1), gate(i2)
    den = g1 + g2
    return grp * EXPERTS_PER_GROUP + i1, grp * EXPERTS_PER_GROUP + i2, g1 / den, g2 / den


def _outproj_kernel(pa_ref, pb_ref, pc_ref, pd_ref, x_ref, w_ref, g1_ref, gam_ref, sh_ref, sc_ref, rw_ref, rb_ref,
                    xo_ref, h_ref, e_ref, gt_ref):
    acc = _dot(pa_ref[...], w_ref[0:MIX_W])
    for i, p_ref in enumerate((pb_ref, pc_ref, pd_ref), start=1):
        acc += _dot(p_ref[...], w_ref[MIX_W * i:MIX_W * (i + 1)])
    x = x_ref[...] + g1_ref[0] * acc
    xo_ref[...] = x
    h = _rms(x, gam_ref[...]) * (1.0 + sc_ref[0]) + sh_ref[0]
    h_ref[...] = h
    logits = _dot3(h, rw_ref[...])
    e1, e2, g1, g2 = _route(logits.T[:32], rb_ref[...])
    rows = lax.broadcasted_iota(jnp.int32, e_ref.shape, 0)
    e_ref[...] = jnp.where(rows == 0, e1, jnp.where(rows == 1, e2, 0))
    gt_ref[...] = jnp.where(rows == 0, g1, jnp.where(rows == 1, g2, 0.0))


def out_projection(parts, tok, w_out, gate1, gamma, shift, scale, rw, rb, n_rows, lat_tiles_per_batch, n_batch, tm):
    d = tok.shape[1]
    mod_idx = lambda i: (jnp.minimum(i // lat_tiles_per_batch, n_batch), 0, 0)
    row = lambda w: pl.BlockSpec((tm, w), lambda i: (i, 0))
    mod = lambda: pl.BlockSpec((1, 1, d), mod_idx)
    return pl.pallas_call(
        _outproj_kernel,
        out_shape=[jax.ShapeDtypeStruct((n_rows, d), F32), jax.ShapeDtypeStruct((n_rows, d), F32),
                   jax.ShapeDtypeStruct((8, n_rows), jnp.int32), jax.ShapeDtypeStruct((8, n_rows), F32)],
        grid=(n_rows // tm,),
        in_specs=[row(MIX_W), row(MIX_W), row(MIX_W), row(MIX_W), row(d),
                  pl.BlockSpec(w_out.shape, lambda i: (0, 0), pipeline_mode=pl.Buffered(1)),
                  mod(), _const_spec((1, d)), mod(), mod(), _const_spec(rw.shape), _const_spec(rb.shape)],
        out_specs=[row(d), row(d), pl.BlockSpec((8, tm), lambda i: (0, i)), pl.BlockSpec((8, tm), lambda i: (0, i))],
        compiler_params=_cp(("parallel",)),
        name="out_projection",
    )(*parts, tok, w_out, gate1, gamma.reshape(1, d), shift, scale, rw, rb)


def _moe_kernel(be_ref, nu_ref, src_ref, src_next_ref, dst_ref, h_hbm, wg_ref, wu_ref, wd_ref, out_hbm,
                xbuf, ybuf, gsem, ssem):
    i = pl.program_id(0)
    n_used = nu_ref[0]
    slot = i % 2

    def gather_start(idx_ref, s):
        def body(r, carry):
            pltpu.make_async_copy(h_hbm.at[pl.ds(idx_ref[0, 0, r], 1)], xbuf.at[s, pl.ds(r, 1)], gsem.at[s]).start()
            return carry
        lax.fori_loop(0, MOE_BLOCK, body, 0, unroll=8)

    def gather_wait(s):
        pltpu.make_async_copy(h_hbm.at[pl.ds(0, MOE_BLOCK)], xbuf.at[s], gsem.at[s]).wait()

    def scatter_start(s):
        def body(r, carry):
            pltpu.make_async_copy(ybuf.at[s, pl.ds(r, 1)], out_hbm.at[pl.ds(dst_ref[0, 0, r], 1)], ssem.at[s]).start()
            return carry
        lax.fori_loop(0, MOE_BLOCK, body, 0, unroll=8)

    def scatter_wait(s):
        pltpu.make_async_copy(ybuf.at[s], out_hbm.at[pl.ds(0, MOE_BLOCK)], ssem.at[s]).wait()

    @pl.when(i < n_used)
    def _():
        @pl.when(i == 0)
        def _():
            gather_start(src_ref, 0)
            ybuf[1] = jnp.zeros(ybuf.shape[1:], F32)
            tail = out_hbm.at[pl.ds(out_hbm.shape[0] - MOE_BLOCK, MOE_BLOCK)]
            pltpu.make_async_copy(ybuf.at[1], tail, ssem.at[1]).start()
            pltpu.make_async_copy(ybuf.at[1], tail, ssem.at[1]).wait()

        @pl.when(i + 1 < n_used)
        def _():
            gather_start(src_next_ref, 1 - slot)

        gather_wait(slot)
        x = xbuf[slot].astype(BF16)
        a = (jax.nn.silu(_dot(x, wg_ref[0])) * _dot(x, wu_ref[0])).astype(BF16)
        y = _dot(a, wd_ref[0])

        @pl.when(i >= 2)
        def _():
            scatter_wait(slot)

        ybuf[slot] = y
        scatter_start(slot)

        @pl.when(i == n_used - 1)
        def _():
            @pl.when(i >= 1)
            def _():
                scatter_wait(1 - slot)

            scatter_wait(slot)


def moe_experts(h, row_src, row_dst, block_expert, n_used, wg, wu, wd, n_out_rows):
    d = h.shape[1]
    n_blocks = row_src.shape[0]
    de = wg.shape[2]
    last = n_blocks - 1
    smem = lambda imap: pl.BlockSpec((1, 1, MOE_BLOCK), imap, memory_space=pltpu.SMEM)
    grid_spec = pltpu.PrefetchScalarGridSpec(
        num_scalar_prefetch=2,
        grid=(n_blocks,),
        in_specs=[smem(lambda i, be, nu: (i, 0, 0)), smem(lambda i, be, nu: (jnp.minimum(i + 1, last), 0, 0)),
                  smem(lambda i, be, nu: (i, 0, 0)),
                  pl.BlockSpec(memory_space=pl.ANY),
                  pl.BlockSpec((1, d, de), lambda i, be, nu: (be[i], 0, 0)), pl.BlockSpec((1, d, de), lambda i, be, nu: (be[i], 0, 0)),
                  pl.BlockSpec((1, de, d), lambda i, be, nu: (be[i], 0, 0))],
        out_specs=pl.BlockSpec(memory_space=pl.ANY),
        scratch_shapes=[pltpu.VMEM((2, MOE_BLOCK, d), F32), pltpu.VMEM((2, MOE_BLOCK, d), F32),
                        pltpu.SemaphoreType.DMA((2,)), pltpu.SemaphoreType.DMA((2,))],
    )
    return pl.pallas_call(
        _moe_kernel,
        out_shape=jax.ShapeDtypeStruct((n_out_rows, d), F32),
        grid_spec=grid_spec,
        compiler_params=_cp(("arbitrary",)),
        name="moe_experts",
    )(block_expert, n_used, row_src, row_src, row_dst, h, wg, wu, wd)


def routed_moe(h, eid, gate, wg, wu, wd):
    t = h.shape[0]
    bm = MOE_BLOCK
    flat_e = eid[:2].reshape(-1)
    hot = (flat_e[:, None] == jnp.arange(N_EXPERTS, dtype=jnp.int32)[None, :]).astype(jnp.int32)
    csum = jnp.cumsum(hot, axis=0)
    counts = csum[-1]
    padded = (counts + bm - 1) // bm * bm
    pad_end = jnp.cumsum(padded)
    pad_start = pad_end - padded
    dest = jnp.sum(hot * (pad_start[None, :] + csum - 1), axis=1)
    n_blocks = (2 * t + N_EXPERTS * (bm - 1) + bm - 1) // bm
    n_rows = n_blocks * bm
    pad_dst = 2 * t + jnp.arange(n_rows, dtype=jnp.int32) % bm
    row_dst = pad_dst.at[dest].set(jnp.arange(2 * t, dtype=jnp.int32))
    row_src = jnp.where(row_dst < 2 * t, row_dst % t, 0)
    block_expert = jnp.minimum(jnp.searchsorted(pad_end, jnp.arange(n_blocks, dtype=jnp.int32) * bm, side='right'),
                               N_EXPERTS - 1).astype(jnp.int32)
    n_used = (pad_end[-1] // bm).astype(jnp.int32).reshape(1)
    yb = moe_experts(h, row_src.reshape(n_blocks, 1, bm), row_dst.reshape(n_blocks, 1, bm), block_expert, n_used,
                     wg, wu, wd, 2 * t + bm)
    return yb, gate[0].reshape(t, 1), gate[1].reshape(t, 1)


def _moe_residual(x_ref, y0_ref, y1_ref, w0_ref, w1_ref, g_ref):
    return x_ref[...] + g_ref[0] * (w0_ref[...] * y0_ref[...] + w1_ref[...] * y1_ref[...])


def _residual_kernel(x_ref, y0_ref, y1_ref, w0_ref, w1_ref, g_ref, o_ref):
    o_ref[...] = _moe_residual(x_ref, y0_ref, y1_ref, w0_ref, w1_ref, g_ref)


def _final_kernel(x_ref, y0_ref, y1_ref, w0_ref, w1_ref, g_ref, gam_ref, o_ref):
    o_ref[...] = _rms(_moe_residual(x_ref, y0_ref, y1_ref, w0_ref, w1_ref, g_ref), gam_ref[...])


def gated_residual(x, moe_out, gate, lat_tiles_per_batch, n_batch, tm, final_gamma=None):
    t, d = x.shape
    yb, w0, w1 = moe_out
    second = w0.shape[0] // tm
    mod_idx = lambda i: (jnp.minimum(i // lat_tiles_per_batch, n_batch), 0, 0)
    row = lambda: pl.BlockSpec((tm, d), lambda i: (i, 0))
    col = lambda: pl.BlockSpec((tm, 1), lambda i: (i, 0))
    in_specs = [row(), row(), pl.BlockSpec((tm, d), lambda i: (second + i, 0)), col(), col(), pl.BlockSpec((1, 1, d), mod_idx)]
    args = [x, yb, yb, w0, w1, gate]
    body = _residual_kernel
    if final_gamma is not None:
        in_specs.append(_const_spec((1, d)))
        args.append(final_gamma.reshape(1, d))
        body = _final_kernel
    return pl.pallas_call(
        body,
        out_shape=jax.ShapeDtypeStruct((t, d), F32),
        grid=(t // tm,),
        in_specs=in_specs,
        out_specs=row(),
        compiler_params=_cp(("parallel",)),
        name="gated_residual",
    )(*args)


def _rope_tables(seq, ctx_len):
    pos = jnp.arange(seq)
    row = (pos // GRID_W).astype(F32)
    col = (pos % GRID_W).astype(F32)

    def tab(dim):
        quarter = dim // 4
        inv = ROPE_THETA ** (-jnp.arange(quarter, dtype=F32) / quarter)
        ang = jnp.concatenate([row[:, None] * inv, col[:, None] * inv], axis=-1)
        cos, sin = jnp.cos(ang), jnp.sin(ang)
        cos_f = jnp.concatenate([cos, cos], axis=-1)
        sin_f = jnp.concatenate([-sin, sin], axis=-1)
        cos_f = jnp.concatenate([cos_f, jnp.ones((ctx_len, dim), F32)], axis=0)
        sin_f = jnp.concatenate([sin_f, jnp.zeros((ctx_len, dim), F32)], axis=0)
        pad = LANE - dim
        return jnp.pad(cos_f, ((0, 0), (0, pad))), jnp.pad(sin_f, ((0, 0), (0, pad)))

    cm, sm = tab(MLA_ROPE)
    cg, sg = tab(GQA_HD)
    return cm, sm, cg, sg


def _half_swap(w, dim):
    return jnp.concatenate([w[..., dim // 2:], w[..., :dim // 2]], axis=-1)


def _layout_w_in(w):
    d = w.shape[0]
    k_rot = w[:, MLA_Q_RANK + MLA_KV_RANK:MLA_IN]
    z = jnp.zeros((d, LANE - MLA_ROPE), w.dtype)
    return jnp.concatenate([w[:, :MLA_Q_RANK + MLA_KV_RANK], k_rot, z, _half_swap(k_rot, MLA_ROPE), z, w[:, OFF_B:]], axis=1).astype(BF16)


def _layout_w_uq(w):
    r = w.shape[0]
    wh = w.reshape(r, MLA_HEADS, MLA_NOPE + MLA_ROPE)
    rope = wh[..., MLA_NOPE:]
    z = jnp.zeros((r, MLA_HEADS, LANE - MLA_ROPE), w.dtype)
    main = jnp.concatenate([wh[..., :MLA_NOPE], rope, z], axis=-1).reshape(r, MLA_HEADS * MLA_DQ)
    rot = jnp.concatenate([_half_swap(rope, MLA_ROPE), z], axis=-1).reshape(r, MLA_HEADS * LANE)
    return jnp.concatenate([main, rot], axis=1).astype(BF16)


def _layout_router(router_w, router_bias):
    d = router_w.shape[0]
    w4 = router_w.reshape(d, N_EXPERT_GROUPS, EXPERTS_PER_GROUP).transpose(0, 2, 1)
    w = jnp.pad(w4, ((0, 0), (0, 0), (0, 8 - N_EXPERT_GROUPS))).reshape(d, 8 * EXPERTS_PER_GROUP)
    w = jnp.pad(w, ((0, 0), (0, LANE - 8 * EXPERTS_PER_GROUP))).astype(F32)
    b4 = router_bias.astype(F32).reshape(N_EXPERT_GROUPS, EXPERTS_PER_GROUP).T
    b = jnp.pad(b4, ((0, 0), (0, 8 - N_EXPERT_GROUPS))).reshape(8 * EXPERTS_PER_GROUP, 1)
    return w, b


def kernel(x, c, ctx, c_ctx, ada_w, ada_b, norm1_g, norm2_g, w_in, mla_q_norm_g, mla_w_uq, mla_kv_norm_g, mla_w_ukv, gqa_q_norm_g, gqa_k_norm_g, ssm_lam_re, ssm_lam_im, ssm_log_dt, ssm_b_re, ssm_b_im, ssm_c_re, ssm_c_im, ssm_d, ssm_w_glu, w_out, router_w, router_bias, moe_w_gate, moe_w_up, moe_w_down, final_norm_g):
    nb, seq, d = x.shape
    ctx_len = ctx.shape[1]
    depth = ada_w.shape[0]
    n_lat, n_ctx = nb * seq, nb * ctx_len
    tm = min(512, n_ctx)
    lat_tiles = seq // tm
    rows_q = min(1024, seq)
    tk_lat = min(4096, seq)
    in_widths = (768, GQA_IN, MIX_W, MIX_W)

    tok = jnp.concatenate([x.reshape(n_lat, d), ctx.reshape(n_ctx, d)], axis=0)
    silu = jnp.concatenate([jax.nn.silu(c), jnp.broadcast_to(jax.nn.silu(c_ctx)[None], (8 - nb, d))], axis=0)
    rope_tabs = _rope_tables(seq, ctx_len)
    rw, rb = _layout_router(router_w, router_bias)
    wg_bf, wu_bf, wd_bf = cast_bf16(moe_w_gate), cast_bf16(moe_w_up), cast_bf16(moe_w_down)
    fft_lat = fft_tables(seq)
    dft_ctx = dft_tables(ctx_len)

    for l in range(depth):
        need_ctx = l < depth - 1
        mod = mod_vectors(silu, ada_w[l], ada_b[l])[:nb + 1].reshape(nb + 1, 1, 6, d)
        sh1, sc1, g1, sh2, sc2, g2 = (mod[:, :, i] for i in range(6))

        pa, pb, pc, pd = in_projection(tok, norm1_g[l], sh1, sc1, _layout_w_in(w_in[l]), in_widths, lat_tiles, nb, tm)

        qm, km, vm, qg, kg, vg = qkv_prepare(pa, pb, rope_tabs, mla_q_norm_g[l], mla_kv_norm_g[l], gqa_q_norm_g[l],
                                             gqa_k_norm_g[l], _layout_w_uq(mla_w_uq[l]), mla_w_ukv[l].astype(BF16),
                                             nb, seq, ctx_len, min(256, ctx_len))
        qm5 = qm.reshape(nb, MLA_HEADS, 1, seq + ctx_len, MLA_DQ)
        out_a = flash_attention(qm5, km, vm, seq, ctx_len, rows_q, tk_lat).reshape(n_lat, MIX_W)
        out_b = flash_attention(qg, kg, vg, seq, ctx_len, rows_q // qg.shape[2], tk_lat).reshape(n_lat, MIX_W)

        s5_tabs = s5_tables(ssm_lam_re[l], ssm_lam_im[l], ssm_log_dt[l], ssm_b_re[l], ssm_b_im[l], ssm_c_re[l], ssm_c_im[l], ssm_d[l])
        out_c = s5_mixer(pc, s5_tabs, ssm_w_glu[l].astype(BF16), nb, seq, ctx_len)

        out_d = fourier_mixer_fft(pd[:n_lat].reshape(nb, seq, MIX_W), fft_lat).reshape(n_lat, MIX_W)

        if need_ctx:
            ctx_a = context_attention(qm5, km, vm, seq, ctx_len).reshape(n_ctx, MIX_W)
            ctx_b = context_attention(qg, kg, vg, seq, ctx_len).reshape(n_ctx, MIX_W)
            ctx_d = fourier_mixer(pd[n_lat:].reshape(nb, ctx_len, MIX_W), dft_ctx, ctx_len, ctx_len).reshape(n_ctx, MIX_W)
            parts = [jnp.concatenate([out_a, ctx_a], axis=0), jnp.concatenate([out_b, ctx_b], axis=0), out_c,
                     jnp.concatenate([out_d, ctx_d], axis=0)]
            n_rows = n_lat + n_ctx
        else:
            parts = [out_a, out_b, out_c, out_d]
            n_rows = n_lat

        tok, h2, eid, gate = out_projection(parts, tok, w_out[l].astype(BF16), g1, norm2_g[l], sh2, sc2, rw, rb,
                                            n_rows, 2 * lat_tiles, nb, tm // 2)
        y = routed_moe(h2, eid, gate, wg_bf[l], wu_bf[l], wd_bf[l])
        tok = gated_residual(tok, y, g2, lat_tiles, nb, tm, None if need_ctx else final_norm_g)

    return tok[:n_lat].reshape(nb, seq, d)
```

```python
import functools
import math

import jax
import jax.numpy as jnp
from jax import lax
from jax.experimental import pallas as pl
from jax.experimental.pallas import tpu as pltpu

F32 = jnp.float32
BF16 = jnp.bfloat16

D_MODEL = 2048
GRID_W = 64
ROPE_THETA = 10000.0
RMS_EPS = 1e-6
MIX_W = D_MODEL // 4
MLA_HEADS = 4
MLA_NOPE = 128
MLA_V = 128
MLA_ROPE = 64
MLA_Q_RANK = 384
MLA_KV_RANK = 128
MLA_IN = MLA_Q_RANK + MLA_KV_RANK + MLA_ROPE
MLA_DQ = 256
GQA_HEADS = 4
GQA_KV_HEADS = 2
GQA_HD = 128
GQA_IN = (GQA_HEADS + 2 * GQA_KV_HEADS) * GQA_HD
SSM_GROUPS = 32
SSM_CH = 16
SSM_STATE = 64
SSM_CHUNK = 16
FNO_GROUPS = 4
FNO_CH = 128
OFF_B = MLA_IN
OFF_C = OFF_B + GQA_IN
OFF_D = OFF_C + MIX_W
N_EXPERTS = 16
N_EXPERT_GROUPS = 4
EXPERTS_PER_GROUP = 4
D_EXPERT = D_MODEL // 2
MOE_BLOCK = 256
LANE = 128
MXU_W = 256
LOG2E = 1.4426950408889634
VMEM_LIMIT = 52 * 1024 * 1024


def _cp(sem, vmem=VMEM_LIMIT):
    return pltpu.CompilerParams(dimension_semantics=sem, vmem_limit_bytes=vmem)


def _const_spec(shape):
    nd = len(shape)
    return pl.BlockSpec(shape, lambda *_: (0,) * nd)


def _split_bf16(a):
    hi = a.astype(BF16)
    lo = (a - hi.astype(F32)).astype(BF16)
    return hi, lo


def _dot(a, b):
    return jnp.dot(a, b, preferred_element_type=F32)


def _dot3(a, b):
    ah, al = _split_bf16(a)
    bh, bl = _split_bf16(b)
    return _dot(ah, bh) + _dot(al, bh) + _dot(ah, bl)


def _rms(x, g):
    return x * lax.rsqrt(jnp.mean(x * x, axis=-1, keepdims=True) + RMS_EPS) * g


def _cast_kernel(x_ref, o_ref):
    o_ref[...] = x_ref[...].astype(o_ref.dtype)


def cast_bf16(w, rows_per_block=1024):
    c = w.shape[-1]
    r = w.size // c
    out = pl.pallas_call(
        _cast_kernel,
        out_shape=jax.ShapeDtypeStruct((r, c), BF16),
        grid=(r // rows_per_block,),
        in_specs=[pl.BlockSpec((rows_per_block, c), lambda i: (i, 0))],
        out_specs=pl.BlockSpec((rows_per_block, c), lambda i: (i, 0)),
        compiler_params=_cp(("parallel",)),
        name="cast_bf16",
    )(w.reshape(r, c))
    return out.reshape(w.shape)


def _mod_kernel(s_ref, w_ref, b_ref, o_ref):
    o_ref[...] = _dot3(s_ref[...], w_ref[...]) + b_ref[...]


def mod_vectors(s8, w, b):
    k, n = w.shape
    tn = 1024
    return pl.pallas_call(
        _mod_kernel,
        out_shape=jax.ShapeDtypeStruct((8, n), F32),
        grid=(n // tn,),
        in_specs=[_const_spec((8, k)), pl.BlockSpec((k, tn), lambda j: (0, j)), pl.BlockSpec((1, tn), lambda j: (0, j))],
        out_specs=pl.BlockSpec((8, tn), lambda j: (0, j)),
        compiler_params=_cp(("parallel",)),
        name="mod_vectors",
    )(s8, w, b.reshape(1, n))


def _inproj_kernel(x_ref, g_ref, sh_ref, sc_ref, w_ref, *o_refs, widths):
    h = _rms(x_ref[...], g_ref[...])
    hb = (h * (1.0 + sc_ref[0]) + sh_ref[0]).astype(BF16)
    off = 0
    for o_ref, wd in zip(o_refs, widths):
        o_ref[...] = _dot(hb, w_ref[:, off:off + wd]).astype(o_ref.dtype)
        off += wd


def in_projection(tok, gamma, shift, scale, w, widths, lat_tiles_per_batch, n_batch, tm):
    t, d = tok.shape
    mod_idx = lambda i: (jnp.minimum(i // lat_tiles_per_batch, n_batch), 0, 0)
    return pl.pallas_call(
        functools.partial(_inproj_kernel, widths=widths),
        out_shape=[jax.ShapeDtypeStruct((t, wd), BF16) for wd in widths],
        grid=(t // tm,),
        in_specs=[pl.BlockSpec((tm, d), lambda i: (i, 0)), _const_spec((1, d)),
                  pl.BlockSpec((1, 1, d), mod_idx), pl.BlockSpec((1, 1, d), mod_idx),
                  pl.BlockSpec(w.shape, lambda i: (0, 0), pipeline_mode=pl.Buffered(1))],
        out_specs=[pl.BlockSpec((tm, wd), lambda i: (i, 0)) for wd in widths],
        compiler_params=_cp(("parallel",)),
        name="in_projection",
    )(tok, gamma.reshape(1, d), shift, scale, w)


def _qkv_kernel(a_ref, b_ref, cm_ref, sm_ref, cg_ref, sg_ref, gq_ref, gkv_ref, ggq_ref, ggk_ref, wuq_ref, wukv_ref,
                qm_ref, km_ref, vm_ref, qg_ref, kg_ref, vg_ref):
    a = a_ref[...].astype(F32)
    cm, sm = cm_ref[...], sm_ref[...]
    mla_scale = LOG2E * (MLA_NOPE + MLA_ROPE) ** -0.5
    qn = _rms(a[:, :MLA_Q_RANK], gq_ref[...]).astype(BF16)
    q_all = _dot(qn, wuq_ref[...])
    rot0 = MLA_HEADS * MLA_DQ
    for h in range(MLA_HEADS):
        nope = q_all[:, MLA_DQ * h:MLA_DQ * h + LANE]
        rp = q_all[:, MLA_DQ * h + LANE:MLA_DQ * (h + 1)]
        rr = q_all[:, rot0 + LANE * h:rot0 + LANE * (h + 1)]
        qm_ref[0, h, :, :LANE] = (nope * mla_scale).astype(BF16)
        qm_ref[0, h, :, LANE:] = ((rp * cm + rr * sm) * mla_scale).astype(BF16)
    kvn = _rms(a[:, MLA_Q_RANK:MLA_Q_RANK + MLA_KV_RANK], gkv_ref[...]).astype(BF16)
    kv = _dot(kvn, wukv_ref[...])
    kr = (a[:, 512:640] * cm + a[:, 640:768] * sm).astype(BF16)
    for h in range(MLA_HEADS):
        km_ref[0, h, :, :LANE] = kv[:, 256 * h:256 * h + LANE].astype(BF16)
        km_ref[0, h, :, LANE:] = kr
        vm_ref[0, h] = kv[:, 256 * h + LANE:256 * (h + 1)].T.astype(BF16)
    b = b_ref[...].astype(F32)
    cg, sg = cg_ref[...], sg_ref[...]
    gqa_scale = LOG2E * GQA_HD ** -0.5
    grp = GQA_HEADS // GQA_KV_HEADS
    for h in range(GQA_HEADS):
        qh = _rms(b[:, GQA_HD * h:GQA_HD * (h + 1)], ggq_ref[...])
        qh = qh * cg + pltpu.roll(qh, GQA_HD // 2, 1) * sg
        qg_ref[0, h // grp, h % grp] = (qh * gqa_scale).astype(BF16)
    k0 = GQA_HEADS * GQA_HD
    v0 = k0 + GQA_KV_HEADS * GQA_HD
    for h in range(GQA_KV_HEADS):
        kh = _rms(b[:, k0 + GQA_HD * h:k0 + GQA_HD * (h + 1)], ggk_ref[...])
        kg_ref[0, h] = (kh * cg + pltpu.roll(kh, GQA_HD // 2, 1) * sg).astype(BF16)
        vg_ref[0, h] = b[:, v0 + GQA_HD * h:v0 + GQA_HD * (h + 1)].T.astype(BF16)

def qkv_prepare(pa, pb, tabs, g_q, g_kv, g_gq, g_gk, w_uq, w_ukv, n_batch, seq, ctx_len, tm):
    t = pa.shape[0]
    n = seq + ctx_len
    lat_tiles = seq // tm
    ctx_tiles = ctx_len // tm
    n_lat = n_batch * lat_tiles

    def bidx(i):
        j = i - n_lat
        bi = jnp.where(i < n_lat, i // lat_tiles, j // ctx_tiles)
        blk = jnp.where(i < n_lat, i % lat_tiles, lat_tiles + j % ctx_tiles)
        return bi, blk

    def o4(i):
        bi, blk = bidx(i)
        return (bi, 0, blk, 0)

    def o4t(i):
        bi, blk = bidx(i)
        return (bi, 0, 0, blk)

    def o5(i):
        bi, blk = bidx(i)
        return (bi, 0, 0, blk, 0)

    tab_spec = pl.BlockSpec((tm, LANE), lambda i: (bidx(i)[1], 0))
    vec = lambda w: _const_spec((1, w))
    return pl.pallas_call(
        _qkv_kernel,
        out_shape=[jax.ShapeDtypeStruct((n_batch, MLA_HEADS, n, MLA_DQ), BF16),
                   jax.ShapeDtypeStruct((n_batch, MLA_HEADS, n, MLA_DQ), BF16),
                   jax.ShapeDtypeStruct((n_batch, MLA_HEADS, MLA_V, n), BF16),
                   jax.ShapeDtypeStruct((n_batch, GQA_KV_HEADS, GQA_HEADS // GQA_KV_HEADS, n, GQA_HD), BF16),
                   jax.ShapeDtypeStruct((n_batch, GQA_KV_HEADS, n, GQA_HD), BF16),
                   jax.ShapeDtypeStruct((n_batch, GQA_KV_HEADS, GQA_HD, n), BF16)],
        grid=(t // tm,),
        in_specs=[pl.BlockSpec((tm, pa.shape[1]), lambda i: (i, 0)), pl.BlockSpec((tm, pb.shape[1]), lambda i: (i, 0)),
                  tab_spec, tab_spec, tab_spec, tab_spec,
                  vec(MLA_Q_RANK), vec(MLA_KV_RANK), vec(GQA_HD), vec(GQA_HD),
                  _const_spec(w_uq.shape), _const_spec(w_ukv.shape)],
        out_specs=[pl.BlockSpec((1, MLA_HEADS, tm, MLA_DQ), o4), pl.BlockSpec((1, MLA_HEADS, tm, MLA_DQ), o4),
                   pl.BlockSpec((1, MLA_HEADS, MLA_V, tm), o4t),
                   pl.BlockSpec((1, GQA_KV_HEADS, GQA_HEADS // GQA_KV_HEADS, tm, GQA_HD), o5),
                   pl.BlockSpec((1, GQA_KV_HEADS, tm, GQA_HD), o4), pl.BlockSpec((1, GQA_KV_HEADS, GQA_HD, tm), o4t)],
        compiler_params=_cp(("parallel",)),
        name="qkv_prepare",
    )(pa, pb, *tabs, g_q.reshape(1, -1), g_kv.reshape(1, -1), g_gq.reshape(1, -1), g_gk.reshape(1, -1), w_uq, w_ukv)


def _dot_nt(a, b):
    return lax.dot_general(a, b, (((1,), (1,)), ((), ())), preferred_element_type=F32)


def _softmax_cols(s):
    m = jnp.max(s, axis=0, keepdims=True)
    p = jnp.exp2(s - m)
    return m, p, jnp.sum(p, axis=0, keepdims=True)


def _store_heads(o_ref, o_t, grp, tq, dv):
    o = o_t.T
    for g in range(grp):
        o_ref[0, :, dv * g:dv * (g + 1)] = o[tq * g:tq * (g + 1)].astype(o_ref.dtype)


def _flash_kernel(q_ref, k_ref, vt_ref, kc_ref, vtc_ref, o_ref, m_sc, l_sc, acc_sc, *, grp, tq, dv, n_kv):
    j = pl.program_id(3)
    rows = grp * tq
    q = q_ref[0, 0].reshape(rows, q_ref.shape[-1])

    @pl.when(j == 0)
    def _():
        m, p, l = _softmax_cols(_dot_nt(kc_ref[0, 0], q))
        m_sc[...] = m
        l_sc[...] = l
        acc_sc[...] = _dot(vtc_ref[0, 0], p.astype(BF16))

    k = k_ref[0, 0]
    vt = vt_ref[0, 0]
    slabs = [slice(MXU_W * r, MXU_W * (r + 1)) for r in range(rows // MXU_W)]
    scores = [_dot_nt(k, q[sl]) for sl in slabs]
    for sl, s in zip(slabs, scores):
        m_prev = m_sc[:, sl]
        m_new = jnp.maximum(m_prev, jnp.max(s, axis=0, keepdims=True))
        alpha = jnp.exp2(m_prev - m_new)
        p = jnp.exp2(s - m_new)
        l_sc[:, sl] = alpha * l_sc[:, sl] + jnp.sum(p, axis=0, keepdims=True)
        acc_sc[:, sl] = alpha * acc_sc[:, sl] + _dot(vt, p.astype(BF16))
        m_sc[:, sl] = m_new

    @pl.when(j == n_kv - 1)
    def _():
        _store_heads(o_ref, acc_sc[...] / l_sc[...], grp, tq, dv)


def flash_attention(q, k, vt, seq, ctx_len, tq, tk):
    nb, hk, grp, n, dq = q.shape
    dv = vt.shape[-2]
    n_kv = seq // tk
    cb = seq // ctx_len
    rows = grp * tq
    return pl.pallas_call(
        functools.partial(_flash_kernel, grp=grp, tq=tq, dv=dv, n_kv=n_kv),
        out_shape=jax.ShapeDtypeStruct((nb, seq, hk * grp * dv), BF16),
        grid=(nb, hk, seq // tq, n_kv),
        in_specs=[pl.BlockSpec((1, 1, grp, tq, dq), lambda b, h, i, j: (b, h, 0, i, 0)),
                  pl.BlockSpec((1, 1, tk, dq), lambda b, h, i, j: (b, h, j, 0)),
                  pl.BlockSpec((1, 1, dv, tk), lambda b, h, i, j: (b, h, 0, j)),
                  pl.BlockSpec((1, 1, ctx_len, dq), lambda b, h, i, j: (b, h, cb, 0)),
                  pl.BlockSpec((1, 1, dv, ctx_len), lambda b, h, i, j: (b, h, 0, cb))],
        out_specs=pl.BlockSpec((1, tq, grp * dv), lambda b, h, i, j: (b, i, h)),
        scratch_shapes=[pltpu.VMEM((1, rows), F32), pltpu.VMEM((1, rows), F32), pltpu.VMEM((dv, rows), F32)],
        compiler_params=_cp(("parallel", "parallel", "parallel", "arbitrary")),
        name="flash_attention",
    )(q, k, vt, k, vt)


def _ctx_attn_kernel(q_ref, kc_ref, vtc_ref, o_ref, *, grp, tq, dv):
    q = q_ref[0, 0].reshape(grp * tq, q_ref.shape[-1])
    _, p, l = _softmax_cols(_dot_nt(kc_ref[0, 0], q))
    _store_heads(o_ref, _dot(vtc_ref[0, 0], p.astype(BF16)) / l, grp, tq, dv)


def context_attention(q, k, vt, seq, ctx_len):
    nb, hk, grp, n, dq = q.shape
    dv = vt.shape[-2]
    cb = seq // ctx_len
    return pl.pallas_call(
        functools.partial(_ctx_attn_kernel, grp=grp, tq=ctx_len, dv=dv),
        out_shape=jax.ShapeDtypeStruct((nb, ctx_len, hk * grp * dv), BF16),
        grid=(nb, hk),
        in_specs=[pl.BlockSpec((1, 1, grp, ctx_len, dq), lambda b, h: (b, h, 0, cb, 0)),
                  pl.BlockSpec((1, 1, ctx_len, dq), lambda b, h: (b, h, cb, 0)),
                  pl.BlockSpec((1, 1, dv, ctx_len), lambda b, h: (b, h, 0, cb))],
        out_specs=pl.BlockSpec((1, ctx_len, grp * dv), lambda b, h: (b, 0, h)),
        compiler_params=_cp(("parallel", "parallel")),
        name="context_attention",
    )(q, k, vt)


def s5_tables(lam_re, lam_im, log_dt, b_re, b_im, c_re, c_im, d):
    hp = lax.Precision.HIGHEST
    tc = SSM_CHUNK
    lam = lax.complex(lam_re.astype(F32), lam_im.astype(F32))
    dt = jnp.exp(log_dt.astype(F32))[..., None]
    lam_bar = jnp.exp(lam * dt)
    b_bar = ((lam_bar - 1.0) / lam)[..., None] * lax.complex(b_re.astype(F32), b_im.astype(F32))
    c_mat = lax.complex(c_re.astype(F32), c_im.astype(F32))
    steps = jnp.arange(tc + 1, dtype=F32)
    lam_pow = jnp.exp((lam * dt)[:, :, None, :] * steps[None, None, :, None])
    kern = jnp.einsum('dgip,dgtp,dgpj->dgtij', c_mat, lam_pow[:, :, :tc], b_bar, precision=hp).real
    s_idx = jnp.arange(tc)[:, None]
    t_idx = jnp.arange(tc)[None, :]
    kf = kern[0][:, jnp.clip(t_idx - s_idx, 0, tc - 1)] * (t_idx >= s_idx)[None, :, :, None, None]
    kb = kern[1][:, jnp.clip(s_idx - t_idx, 0, tc - 1)] * (s_idx >= t_idx)[None, :, :, None, None]
    dg = d.astype(F32).reshape(SSM_GROUPS, SSM_CH)
    skip = (s_idx == t_idx)[None, :, :, None, None] * (jnp.eye(SSM_CH, dtype=F32) * dg[:, :, None])[:, None, None]
    w_t = (kf + kb + skip).transpose(0, 1, 4, 2, 3).reshape(SSM_GROUPS, tc * SSM_CH, tc * SSM_CH)
    zf = lam_pow[0][:, tc - 1 - jnp.arange(tc)][..., None] * b_bar[0][:, None]
    zb = lam_pow[1][:, jnp.arange(tc)][..., None] * b_bar[1][:, None]
    to_rows = lambda z: z.transpose(0, 1, 3, 2).reshape(SSM_GROUPS, tc * SSM_CH, SSM_STATE)
    w_z = jnp.concatenate([to_rows(zf.real), to_rows(zb.real), to_rows(zf.imag), to_rows(zb.imag)], axis=-1)
    mf = c_mat[0][:, None] * lam_pow[0][:, 1 + jnp.arange(tc)][:, :, None, :]
    mb = c_mat[1][:, None] * lam_pow[1][:, tc - jnp.arange(tc)][:, :, None, :]
    to_cols = lambda m: m.transpose(0, 3, 1, 2).reshape(SSM_GROUPS, SSM_STATE, tc * SSM_CH)
    w_c = jnp.concatenate([to_cols(mf.real), to_cols(mb.real), -to_cols(mf.imag), -to_cols(mb.imag)], axis=1)
    a16 = lam_pow[:, :, tc]
    a_re = jnp.concatenate([a16[0].real, a16[1].real], axis=-1)
    a_im = jnp.concatenate([a16[0].imag, a16[1].imag], axis=-1)
    return w_t.astype(BF16), w_z.astype(BF16), w_c.astype(BF16), a_re, a_im


def _s5_z_kernel(u_ref, w_ref, z_ref):
    z_ref[0] = _dot(u_ref[0], w_ref[0])


def s5_chunk_sums(u, w_z):
    bg, kc, cw = u.shape
    return pl.pallas_call(
        _s5_z_kernel,
        out_shape=jax.ShapeDtypeStruct((bg, kc, cw), F32),
        grid=(bg,),
        in_specs=[pl.BlockSpec((1, kc, cw), lambda i: (i, 0, 0)), pl.BlockSpec((1, cw, cw), lambda i: (i % SSM_GROUPS, 0, 0))],
        out_specs=pl.BlockSpec((1, kc, cw), lambda i: (i, 0, 0)),
        compiler_params=_cp(("parallel",)),
        name="s5_chunk_sums",
    )(u, w_z)


def _s5_scan_kernel(z_ref, ar_ref, ai_ref, h_ref, *, k_lat, k_ctx):
    ar, ai = ar_ref[...], ai_ref[...]
    rows = ar.shape[0]
    fwd_lanes = lax.broadcasted_iota(jnp.int32, (rows, LANE), 1) < SSM_STATE

    def step(cf, cb, hr, hi):
        h_ref[cf, :, 0:SSM_STATE] = hr[:, :SSM_STATE]
        h_ref[cb, :, SSM_STATE:LANE] = hr[:, SSM_STATE:]
        h_ref[cf, :, LANE:LANE + SSM_STATE] = hi[:, :SSM_STATE]
        h_ref[cb, :, LANE + SSM_STATE:] = hi[:, SSM_STATE:]
        zf, zb = z_ref[cf], z_ref[cb]
        zr = jnp.where(fwd_lanes, zf[:, :LANE], zb[:, :LANE])
        zi = jnp.where(fwd_lanes, zf[:, LANE:], zb[:, LANE:])
        return ar * hr - ai * hi + zr, ar * hi + ai * hr + zi

    def ctx_body(n, c):
        return step(k_lat + n, k_lat + k_ctx - 1 - n, *c)

    def lat_body(n, c):
        return step(n, k_lat - 1 - n, *c)

    zero = jnp.zeros((rows, LANE), F32)
    carry = lax.fori_loop(0, k_ctx, ctx_body, (zero, zero))
    lax.fori_loop(0, k_lat, lat_body, carry)


def s5_state_scan(z, a_re, a_im, k_lat, k_ctx):
    kc, r, cw = z.shape
    tr = 8
    return pl.pallas_call(
        functools.partial(_s5_scan_kernel, k_lat=k_lat, k_ctx=k_ctx),
        out_shape=jax.ShapeDtypeStruct((kc, r, cw), F32),
        grid=(r // tr,),
        in_specs=[pl.BlockSpec((kc, tr, cw), lambda i: (0, i, 0)), pl.BlockSpec((tr, LANE), lambda i: (i, 0)),
                  pl.BlockSpec((tr, LANE), lambda i: (i, 0))],
        out_specs=pl.BlockSpec((kc, tr, cw), lambda i: (0, i, 0)),
        compiler_params=_cp(("parallel",)),
        name="s5_state_scan",
    )(z, a_re, a_im)


def _gelu_tanh(y):
    return 0.5 * y * (1.0 + jnp.tanh(math.sqrt(2.0 / math.pi) * (y + 0.044715 * (y * y * y))))


def _s5_y_kernel(u_ref, h_ref, wt_ref, wc_ref, y_ref):
    y = _dot(u_ref[0], wt_ref[0]) + _dot(h_ref[0].astype(BF16), wc_ref[0])
    y_ref[0] = _gelu_tanh(y).astype(y_ref.dtype)


def s5_outputs(u, h, w_t, w_c):
    bg, kc, cw = u.shape
    blk = lambda: pl.BlockSpec((1, kc, cw), lambda i: (i, 0, 0))
    wsp = lambda: pl.BlockSpec((1, cw, cw), lambda i: (i % SSM_GROUPS, 0, 0))
    return pl.pallas_call(
        _s5_y_kernel,
        out_shape=jax.ShapeDtypeStruct((bg, kc, cw), BF16),
        grid=(bg,),
        in_specs=[blk(), blk(), wsp(), wsp()],
        out_specs=blk(),
        compiler_params=_cp(("parallel",)),
        name="s5_outputs",
    )(u, h, w_t, w_c)


def _glu_kernel(y_ref, w_ref, o_ref):
    z = _dot(y_ref[...], w_ref[...])
    half = z.shape[1] // 2
    o_ref[...] = (z[:, :half] * jax.nn.sigmoid(z[:, half:])).astype(o_ref.dtype)


def glu_matmul(y, w, tm):
    t, kdim = y.shape
    return pl.pallas_call(
        _glu_kernel,
        out_shape=jax.ShapeDtypeStruct((t, w.shape[1] // 2), BF16),
        grid=(t // tm,),
        in_specs=[pl.BlockSpec((tm, kdim), lambda i: (i, 0)), _const_spec(w.shape)],
        out_specs=pl.BlockSpec((tm, w.shape[1] // 2), lambda i: (i, 0)),
        compiler_params=_cp(("parallel",)),
        name="glu_matmul",
    )(y, w)


def s5_mixer(u_tok, tables, w_glu, n_batch, seq, ctx_len):
    w_t, w_z, w_c, a_re, a_im = tables
    tc = SSM_CHUNK
    n = seq + ctx_len
    kc, k_lat, k_ctx = n // tc, seq // tc, ctx_len // tc
    u_seq = jnp.concatenate([u_tok[:n_batch * seq].reshape(n_batch, seq, MIX_W),
                             u_tok[n_batch * seq:].reshape(n_batch, ctx_len, MIX_W)], axis=1)
    u_ch = u_seq.reshape(n_batch, kc, tc, SSM_GROUPS, SSM_CH).transpose(0, 3, 1, 2, 4).reshape(n_batch * SSM_GROUPS, kc, tc * SSM_CH)
    z = s5_chunk_sums(u_ch, w_z)
    h = s5_state_scan(z.transpose(1, 0, 2), jnp.tile(a_re, (n_batch, 1)), jnp.tile(a_im, (n_batch, 1)), k_lat, k_ctx)
    y = s5_outputs(u_ch, h.transpose(1, 0, 2), w_t, w_c)
    y = y.reshape(n_batch, SSM_GROUPS, kc, tc, SSM_CH).transpose(0, 2, 3, 1, 4).reshape(n_batch, n, MIX_W)
    y_tok = jnp.concatenate([y[:, :seq].reshape(n_batch * seq, MIX_W), y[:, seq:].reshape(n_batch * ctx_len, MIX_W)], axis=0)
    return glu_matmul(y_tok, w_glu, 512)


def _chan_dft_kernel(x_ref, w_ref, yc_ref, ys_ref):
    x = x_ref[...]
    for g in range(FNO_GROUPS):
        y = _dot(x[:, FNO_CH * g:FNO_CH * (g + 1)], w_ref[...])
        yc_ref[:, FNO_CH * g:FNO_CH * (g + 1)] = y[:, :FNO_CH].astype(BF16)
        ys_ref[:, FNO_CH * g:FNO_CH * (g + 1)] = y[:, FNO_CH:].astype(BF16)


def chan_dft(x, w, tm):
    t, cw = x.shape
    return pl.pallas_call(
        _chan_dft_kernel,
        out_shape=[jax.ShapeDtypeStruct((t, cw), BF16)] * 2,
        grid=(t // tm,),
        in_specs=[pl.BlockSpec((tm, cw), lambda i: (i, 0)), _const_spec(w.shape)],
        out_specs=[pl.BlockSpec((tm, cw), lambda i: (i, 0))] * 2,
        compiler_params=_cp(("parallel",)),
        name="chan_dft",
    )(x, w)


def _pos_dft_kernel(c_ref, s_ref, yc_ref, ys_ref, o_ref, acc_ref, *, n_k):
    j = pl.program_id(2)

    @pl.when(j == 0)
    def _():
        acc_ref[...] = jnp.zeros_like(acc_ref)

    acc_ref[...] += _dot(c_ref[...], yc_ref[0]) - _dot(s_ref[...], ys_ref[0])

    @pl.when(j == n_k - 1)
    def _():
        o_ref[0] = acc_ref[...].astype(o_ref.dtype)


def pos_dft(cos_t, sin_t, yc, ys, tm, tk):
    nb, n, w = yc.shape
    n_k = n // tk
    return pl.pallas_call(
        functools.partial(_pos_dft_kernel, n_k=n_k),
        out_shape=jax.ShapeDtypeStruct((nb, n, w), BF16),
        grid=(nb, n // tm, n_k),
        in_specs=[pl.BlockSpec((tm, tk), lambda b, i, j: (i, j)), pl.BlockSpec((tm, tk), lambda b, i, j: (i, j)),
                  pl.BlockSpec((1, tk, w), lambda b, i, j: (b, j, 0)), pl.BlockSpec((1, tk, w), lambda b, i, j: (b, j, 0))],
        out_specs=pl.BlockSpec((1, tm, w), lambda b, i, j: (b, i, 0)),
        scratch_shapes=[pltpu.VMEM((tm, w), F32)],
        compiler_params=_cp(("parallel", "parallel", "arbitrary")),
        name="pos_dft",
    )(cos_t, sin_t, yc, ys)


def dft_tables(n):
    idx = (jnp.arange(n, dtype=jnp.int32)[:, None] * jnp.arange(n, dtype=jnp.int32)[None, :]) % n
    ang = idx.astype(F32) * (2.0 * math.pi / n)
    return jnp.cos(ang).astype(BF16), jnp.sin(ang).astype(BF16)


def chan_table(n_pos):
    c = jnp.arange(FNO_CH, dtype=jnp.int32)
    ang = ((c[:, None] * c[None, :]) % FNO_CH).astype(F32) * (2.0 * math.pi / FNO_CH)
    scale = 1.0 / math.sqrt(n_pos * FNO_CH)
    return (jnp.concatenate([jnp.cos(ang), jnp.sin(ang)], axis=1) * scale).astype(BF16)


def fourier_mixer(x_seq, pos_tabs, tm, tk):
    nb, n, w = x_seq.shape
    yc, ys = chan_dft(x_seq.reshape(nb * n, w), chan_table(n), min(512, n))
    return pos_dft(pos_tabs[0], pos_tabs[1], yc.reshape(nb, n, w), ys.reshape(nb, n, w), tm, tk)


FFT_CH = 8


def fft_tables(n):
    a = n // LANE
    ia = jnp.arange(a, dtype=jnp.int32)
    ib = jnp.arange(LANE, dtype=jnp.int32)
    ang_a = ((ia[:, None] * ia[None, :]) % a).astype(F32) * (2.0 * math.pi / a)
    ang_b = ((ib[:, None] * ib[None, :]) % LANE).astype(F32) * (2.0 * math.pi / LANE)
    ang_t = (ia[:, None] * ib[None, :]).astype(F32) * (2.0 * math.pi / n)
    f_a = jnp.concatenate([jnp.cos(ang_a), jnp.sin(ang_a)], axis=0).astype(BF16)
    return f_a, jnp.cos(ang_b).astype(BF16), jnp.sin(ang_b).astype(BF16), jnp.cos(ang_t), -jnp.sin(ang_t)


def _fft_kernel(yc_ref, ys_ref, fa_ref, cb_ref, sb_ref, tr_ref, ti_ref, o_ref, *, a):
    fa, cb, sb, tr, ti = fa_ref[...], cb_ref[...], sb_ref[...], tr_ref[...], ti_ref[...]
    for j in range(FFT_CH):
        y = _dot(fa, jnp.concatenate([yc_ref[0, j], ys_ref[0, j]], axis=1))
        gr = y[:a, :LANE] - y[a:, LANE:]
        gi = -(y[:a, LANE:] + y[a:, :LANE])
        ar = (gr * tr - gi * ti).astype(BF16)
        ai = (gr * ti + gi * tr).astype(BF16)
        o_ref[0, j] = (_dot_nt(cb, ar) + _dot_nt(sb, ai)).astype(o_ref.dtype)


def fft_positions(yc, ys, tabs):
    nb, w, a, _ = yc.shape
    xin = lambda: pl.BlockSpec((1, FFT_CH, a, LANE), lambda b, c: (b, c, 0, 0))
    return pl.pallas_call(
        functools.partial(_fft_kernel, a=a),
        out_shape=jax.ShapeDtypeStruct((nb, w, LANE, a), BF16),
        grid=(nb, w // FFT_CH),
        in_specs=[xin(), xin()] + [_const_spec(t.shape) for t in tabs],
        out_specs=pl.BlockSpec((1, FFT_CH, LANE, a), lambda b, c: (b, c, 0, 0)),
        compiler_params=_cp(("parallel", "parallel")),
        name="fft_positions",
    )(yc, ys, *tabs)


def fourier_mixer_fft(x_seq, tabs):
    nb, n, w = x_seq.shape
    yc, ys = chan_dft(x_seq.reshape(nb * n, w), chan_table(n), min(512, n))
    chan_major = lambda y: y.reshape(nb, n, w).transpose(0, 2, 1).reshape(nb, w, n // LANE, LANE)
    z = fft_positions(chan_major(yc), chan_major(ys), tabs)
    return z.reshape(nb, w, n).transpose(0, 2, 1)


def _route(logits_t, bias_col):
    score = jax.nn.sigmoid(logits_t)
    sel = score + bias_col
    sa, sb, sc, sd = (sel[8 * i:8 * (i + 1)] for i in range(EXPERTS_PER_GROUP))
    ra, rb, rc, rd = (score[8 * i:8 * (i + 1)] for i in range(EXPERTS_PER_GROUP))
    m1, n1 = jnp.maximum(sa, sb), jnp.minimum(sa, sb)
    m2, n2 = jnp.maximum(sc, sd), jnp.minimum(sc, sd)
    gsum = jnp.maximum(m1, m2) + jnp.maximum(jnp.minimum(m1, m2), jnp.maximum(n1, n2))
    rows = lax.broadcasted_iota(jnp.int32, gsum.shape, 0)
    neg = jnp.float32(-jnp.inf)
    gsum = jnp.where(rows < N_EXPERT_GROUPS, gsum, neg)
    best = jnp.max(gsum, axis=0, keepdims=True)
    grp_f = jnp.min(jnp.where(gsum == best, rows.astype(F32), 8.0), axis=0, keepdims=True)
    grp = grp_f.astype(jnp.int32)
    hot = rows == grp
    pick = lambda v: jnp.sum(jnp.where(hot, v, 0.0), axis=0, keepdims=True)
    a, b, c, d = pick(sa), pick(sb), pick(sc), pick(sd)
    wa, wb, wc, wd = pick(ra), pick(rb), pick(rc), pick(rd)

    def first_max(va, vb, vc, vd):
        m = jnp.maximum(jnp.maximum(va, vb), jnp.maximum(vc, vd))
        return jnp.where(va == m, 0, jnp.where(vb == m, 1, jnp.where(vc == m, 2, 3)))

    i1 = first_max(a, b, c, d)
    i2 = first_max(jnp.where(i1 == 0, neg, a), jnp.where(i1 == 1, neg, b), jnp.where(i1 == 2, neg, c), jnp.where(i1 == 3, neg, d))
    gate = lambda i: jnp.where(i == 0, wa, jnp.where(i == 1, wb, jnp.where(i == 2, wc, wd)))
    g1, g2 = gate(i1), gate(i2)
    den = g1 + g2
    return grp * EXPERTS_PER_GROUP + i1, grp * EXPERTS_PER_GROUP + i2, g1 / den, g2 / den


def _outproj_kernel(pa_ref, pb_ref, pc_ref, pd_ref, x_ref, w_ref, g1_ref, gam_ref, sh_ref, sc_ref, rw_ref, rb_ref,
                    xo_ref, h_ref, e_ref, gt_ref):
    acc = _dot(pa_ref[...], w_ref[0:MIX_W])
    for i, p_ref in enumerate((pb_ref, pc_ref, pd_ref), start=1):
        acc += _dot(p_ref[...], w_ref[MIX_W * i:MIX_W * (i + 1)])
    x = x_ref[...] + g1_ref[0] * acc
    xo_ref[...] = x
    h = _rms(x, gam_ref[...]) * (1.0 + sc_ref[0]) + sh_ref[0]
    h_ref[...] = h
    logits = _dot3(h, rw_ref[...])
    e1, e2, g1, g2 = _route(logits.T[:32], rb_ref[...])
    rows = lax.broadcasted_iota(jnp.int32, e_ref.shape, 0)
    e_ref[...] = jnp.where(rows == 0, e1, jnp.where(rows == 1, e2, 0))
    gt_ref[...] = jnp.where(rows == 0, g1, jnp.where(rows == 1, g2, 0.0))


def out_projection(parts, tok, w_out, gate1, gamma, shift, scale, rw, rb, n_rows, lat_tiles_per_batch, n_batch, tm):
    d = tok.shape[1]
    mod_idx = lambda i: (jnp.minimum(i // lat_tiles_per_batch, n_batch), 0, 0)
    row = lambda w: pl.BlockSpec((tm, w), lambda i: (i, 0))
    mod = lambda: pl.BlockSpec((1, 1, d), mod_idx)
    return pl.pallas_call(
        _outproj_kernel,
        out_shape=[jax.ShapeDtypeStruct((n_rows, d), F32), jax.ShapeDtypeStruct((n_rows, d), F32),
                   jax.ShapeDtypeStruct((8, n_rows), jnp.int32), jax.ShapeDtypeStruct((8, n_rows), F32)],
        grid=(n_rows // tm,),
        in_specs=[row(MIX_W), row(MIX_W), row(MIX_W), row(MIX_W), row(d),
                  pl.BlockSpec(w_out.shape, lambda i: (0, 0), pipeline_mode=pl.Buffered(1)),
                  mod(), _const_spec((1, d)), mod(), mod(), _const_spec(rw.shape), _const_spec(rb.shape)],
        out_specs=[row(d), row(d), pl.BlockSpec((8, tm), lambda i: (0, i)), pl.BlockSpec((8, tm), lambda i: (0, i))],
        compiler_params=_cp(("parallel",)),
        name="out_projection",
    )(*parts, tok, w_out, gate1, gamma.reshape(1, d), shift, scale, rw, rb)


def _moe_kernel(be_ref, nu_ref, src_ref, src_next_ref, dst_prev_ref, h_hbm, wg_ref, wu_ref, wd_ref, out_hbm,
                xbuf0, xbuf1, ybuf0, ybuf1, gsem, ssem):
    i = pl.program_id(0)
    n_used = nu_ref[0]
    n_out = out_hbm.shape[0]
    xbuf = (xbuf0, xbuf1)
    ybuf = (ybuf0, ybuf1)

    def gather_start(idx_ref, s):
        for r in range(MOE_BLOCK):
            pltpu.make_async_copy(h_hbm.at[pl.ds(idx_ref[0, 0, r], 1)], xbuf[s].at[pl.ds(r, 1)], gsem.at[s]).start()

    def gather_wait(s):
        pltpu.make_async_copy(h_hbm.at[pl.ds(0, MOE_BLOCK)], xbuf[s], gsem.at[s]).wait()

    def scatter_prev_start(s):
        for r in range(MOE_BLOCK):
            pltpu.make_async_copy(ybuf[s].at[pl.ds(r, 1)], out_hbm.at[pl.ds(dst_prev_ref[0, 0, r], 1)], ssem.at[s]).start()

    def scatter_wait(s):
        pltpu.make_async_copy(ybuf[s], out_hbm.at[pl.ds(0, MOE_BLOCK)], ssem.at[s]).wait()

    @pl.when(i == 0)
    def _():
        gather_start(src_ref, 0)
        ybuf0[...] = jnp.zeros(ybuf0.shape, F32)
        ybuf1[...] = jnp.zeros(ybuf1.shape, F32)
        pltpu.make_async_copy(ybuf0, out_hbm.at[pl.ds(n_out - 2 * MOE_BLOCK, MOE_BLOCK)], ssem.at[0]).start()

    def compute_step(slot):
        other = 1 - slot
        gather_wait(slot)
        x = xbuf[slot][...].astype(BF16)
        scatter_prev_start(other)
        gather_start(src_next_ref, other)
        a = (jax.nn.silu(_dot(x, wg_ref[0])) * _dot(x, wu_ref[0])).astype(BF16)
        scatter_wait(slot)
        ybuf[slot][...] = _dot(a, wd_ref[0])

    def drain_step(slot):
        other = 1 - slot
        scatter_prev_start(other)
        scatter_wait(other)
        scatter_wait(slot)
        gather_wait(slot)

    for parity in range(2):
        pl.when((i < n_used) & (i % 2 == parity))(functools.partial(compute_step, parity))
        pl.when((i == n_used) & (i % 2 == parity))(functools.partial(drain_step, parity))


def moe_experts(h, row_src, row_dst_prev, block_expert, n_used, wg, wu, wd, n_out_rows):
    d = h.shape[1]
    n_blocks = row_src.shape[0]
    de = wg.shape[2]
    last = n_blocks - 1
    smem = lambda imap: pl.BlockSpec((1, 1, MOE_BLOCK), imap, memory_space=pltpu.SMEM)
    cur = lambda i: jnp.minimum(i, last)
    grid_spec = pltpu.PrefetchScalarGridSpec(
        num_scalar_prefetch=2,
        grid=(n_blocks + 1,),
        in_specs=[smem(lambda i, be, nu: (cur(i), 0, 0)),
                  smem(lambda i, be, nu: (jnp.minimum(i + 1, nu[0] - 1), 0, 0)),
                  smem(lambda i, be, nu: (i, 0, 0)),
                  pl.BlockSpec(memory_space=pl.ANY),
                  pl.BlockSpec((1, d, de), lambda i, be, nu: (be[cur(i)], 0, 0)),
                  pl.BlockSpec((1, d, de), lambda i, be, nu: (be[cur(i)], 0, 0)),
                  pl.BlockSpec((1, de, d), lambda i, be, nu: (be[cur(i)], 0, 0))],
        out_specs=pl.BlockSpec(memory_space=pl.ANY),
        scratch_shapes=[pltpu.VMEM((MOE_BLOCK, d), F32)] * 4 + [
                        pltpu.SemaphoreType.DMA((2,)), pltpu.SemaphoreType.DMA((2,))],
    )
    return pl.pallas_call(
        _moe_kernel,
        out_shape=jax.ShapeDtypeStruct((n_out_rows, d), F32),
        grid_spec=grid_spec,
        compiler_params=_cp(("arbitrary",)),
        name="moe_experts",
    )(block_expert, n_used, row_src, row_src, row_dst_prev, h, wg, wu, wd)


def routed_moe(h, eid, gate, wg, wu, wd):
    t = h.shape[0]
    bm = MOE_BLOCK
    flat_e = eid[:2].reshape(-1)
    hot = (flat_e[:, None] == jnp.arange(N_EXPERTS, dtype=jnp.int32)[None, :]).astype(jnp.int32)
    csum = jnp.cumsum(hot, axis=0)
    counts = csum[-1]
    padded = (counts + bm - 1) // bm * bm
    pad_end = jnp.cumsum(padded)
    pad_start = pad_end - padded
    dest = jnp.sum(hot * (pad_start[None, :] + csum - 1), axis=1)
    n_blocks = (2 * t + N_EXPERTS * (bm - 1) + bm - 1) // bm
    n_rows = n_blocks * bm
    pad_dst = 2 * t + jnp.arange(n_rows, dtype=jnp.int32) % bm
    row_dst = pad_dst.at[dest].set(jnp.arange(2 * t, dtype=jnp.int32))
    row_src = jnp.where(row_dst < 2 * t, row_dst % t, 0)
    block_expert = jnp.minimum(jnp.searchsorted(pad_end, jnp.arange(n_blocks, dtype=jnp.int32) * bm, side='right'),
                               N_EXPERTS - 1).astype(jnp.int32)
    n_used = (pad_end[-1] // bm).astype(jnp.int32).reshape(1)
    spare = 2 * t + bm + jnp.arange(bm, dtype=jnp.int32)
    row_dst_prev = jnp.concatenate([spare, row_dst]).reshape(n_blocks + 1, 1, bm)
    yb = moe_experts(h, row_src.reshape(n_blocks, 1, bm), row_dst_prev, block_expert, n_used, wg, wu, wd, 2 * t + 2 * bm)
    return yb, gate[0].reshape(t, 1), gate[1].reshape(t, 1)


def _moe_residual(x_ref, y0_ref, y1_ref, w0_ref, w1_ref, g_ref):
    return x_ref[...] + g_ref[0] * (w0_ref[...] * y0_ref[...] + w1_ref[...] * y1_ref[...])


def _residual_kernel(x_ref, y0_ref, y1_ref, w0_ref, w1_ref, g_ref, o_ref):
    o_ref[...] = _moe_residual(x_ref, y0_ref, y1_ref, w0_ref, w1_ref, g_ref)


def _final_kernel(x_ref, y0_ref, y1_ref, w0_ref, w1_ref, g_ref, gam_ref, o_ref):
    o_ref[...] = _rms(_moe_residual(x_ref, y0_ref, y1_ref, w0_ref, w1_ref, g_ref), gam_ref[...])


def gated_residual(x, moe_out, gate, lat_tiles_per_batch, n_batch, tm, final_gamma=None):
    t, d = x.shape
    yb, w0, w1 = moe_out
    second = w0.shape[0] // tm
    mod_idx = lambda i: (jnp.minimum(i // lat_tiles_per_batch, n_batch), 0, 0)
    row = lambda: pl.BlockSpec((tm, d), lambda i: (i, 0))
    col = lambda: pl.BlockSpec((tm, 1), lambda i: (i, 0))
    in_specs = [row(), row(), pl.BlockSpec((tm, d), lambda i: (second + i, 0)), col(), col(), pl.BlockSpec((1, 1, d), mod_idx)]
    args = [x, yb, yb, w0, w1, gate]
    body = _residual_kernel
    if final_gamma is not None:
        in_specs.append(_const_spec((1, d)))
        args.append(final_gamma.reshape(1, d))
        body = _final_kernel
    return pl.pallas_call(
        body,
        out_shape=jax.ShapeDtypeStruct((t, d), F32),
        grid=(t // tm,),
        in_specs=in_specs,
        out_specs=row(),
        compiler_params=_cp(("parallel",)),
        name="gated_residual",
    )(*args)


def _rope_tables(seq, ctx_len):
    pos = jnp.arange(seq)
    row = (pos // GRID_W).astype(F32)
    col = (pos % GRID_W).astype(F32)

    def tab(dim):
        quarter = dim // 4
        inv = ROPE_THETA ** (-jnp.arange(quarter, dtype=F32) / quarter)
        ang = jnp.concatenate([row[:, None] * inv, col[:, None] * inv], axis=-1)
        cos, sin = jnp.cos(ang), jnp.sin(ang)
        cos_f = jnp.concatenate([cos, cos], axis=-1)
        sin_f = jnp.concatenate([-sin, sin], axis=-1)
        cos_f = jnp.concatenate([cos_f, jnp.ones((ctx_len, dim), F32)], axis=0)
        sin_f = jnp.concatenate([sin_f, jnp.zeros((ctx_len, dim), F32)], axis=0)
        pad = LANE - dim
        return jnp.pad(cos_f, ((0, 0), (0, pad))), jnp.pad(sin_f, ((0, 0), (0, pad)))

    cm, sm = tab(MLA_ROPE)
    cg, sg = tab(GQA_HD)
    return cm, sm, cg, sg


def _half_swap(w, dim):
    return jnp.concatenate([w[..., dim // 2:], w[..., :dim // 2]], axis=-1)


def _layout_w_in(w):
    d = w.shape[0]
    k_rot = w[:, MLA_Q_RANK + MLA_KV_RANK:MLA_IN]
    z = jnp.zeros((d, LANE - MLA_ROPE), w.dtype)
    return jnp.concatenate([w[:, :MLA_Q_RANK + MLA_KV_RANK], k_rot, z, _half_swap(k_rot, MLA_ROPE), z, w[:, OFF_B:]], axis=1).astype(BF16)


def _layout_w_uq(w):
    r = w.shape[0]
    wh = w.reshape(r, MLA_HEADS, MLA_NOPE + MLA_ROPE)
    rope = wh[..., MLA_NOPE:]
    z = jnp.zeros((r, MLA_HEADS, LANE - MLA_ROPE), w.dtype)
    main = jnp.concatenate([wh[..., :MLA_NOPE], rope, z], axis=-1).reshape(r, MLA_HEADS * MLA_DQ)
    rot = jnp.concatenate([_half_swap(rope, MLA_ROPE), z], axis=-1).reshape(r, MLA_HEADS * LANE)
    return jnp.concatenate([main, rot], axis=1).astype(BF16)


def _layout_router(router_w, router_bias):
    d = router_w.shape[0]
    w4 = router_w.reshape(d, N_EXPERT_GROUPS, EXPERTS_PER_GROUP).transpose(0, 2, 1)
    w = jnp.pad(w4, ((0, 0), (0, 0), (0, 8 - N_EXPERT_GROUPS))).reshape(d, 8 * EXPERTS_PER_GROUP)
    w = jnp.pad(w, ((0, 0), (0, LANE - 8 * EXPERTS_PER_GROUP))).astype(F32)
    b4 = router_bias.astype(F32).reshape(N_EXPERT_GROUPS, EXPERTS_PER_GROUP).T
    b = jnp.pad(b4, ((0, 0), (0, 8 - N_EXPERT_GROUPS))).reshape(8 * EXPERTS_PER_GROUP, 1)
    return w, b


def kernel(x, c, ctx, c_ctx, ada_w, ada_b, norm1_g, norm2_g, w_in, mla_q_norm_g, mla_w_uq, mla_kv_norm_g, mla_w_ukv, gqa_q_norm_g, gqa_k_norm_g, ssm_lam_re, ssm_lam_im, ssm_log_dt, ssm_b_re, ssm_b_im, ssm_c_re, ssm_c_im, ssm_d, ssm_w_glu, w_out, router_w, router_bias, moe_w_gate, moe_w_up, moe_w_down, final_norm_g):
    nb, seq, d = x.shape
    ctx_len = ctx.shape[1]
    depth = ada_w.shape[0]
    n_lat, n_ctx = nb * seq, nb * ctx_len
    tm = min(512, n_ctx)
    lat_tiles = seq // tm
    rows_q = min(1024, seq)
    tk_lat = min(4096, seq)
    in_widths = (768, GQA_IN, MIX_W, MIX_W)

    tok = jnp.concatenate([x.reshape(n_lat, d), ctx.reshape(n_ctx, d)], axis=0)
    silu = jnp.concatenate([jax.nn.silu(c), jnp.broadcast_to(jax.nn.silu(c_ctx)[None], (8 - nb, d))], axis=0)
    rope_tabs = _rope_tables(seq, ctx_len)
    rw, rb = _layout_router(router_w, router_bias)
    wg_bf, wu_bf, wd_bf = cast_bf16(moe_w_gate), cast_bf16(moe_w_up), cast_bf16(moe_w_down)
    fft_lat = fft_tables(seq)
    dft_ctx = dft_tables(ctx_len)

    for l in range(depth):
        need_ctx = l < depth - 1
        mod = mod_vectors(silu, ada_w[l], ada_b[l])[:nb + 1].reshape(nb + 1, 1, 6, d)
        sh1, sc1, g1, sh2, sc2, g2 = (mod[:, :, i] for i in range(6))

        pa, pb, pc, pd = in_projection(tok, norm1_g[l], sh1, sc1, _layout_w_in(w_in[l]), in_widths, lat_tiles, nb, tm)

        qm, km, vm, qg, kg, vg = qkv_prepare(pa, pb, rope_tabs, mla_q_norm_g[l], mla_kv_norm_g[l], gqa_q_norm_g[l],
                                             gqa_k_norm_g[l], _layout_w_uq(mla_w_uq[l]), mla_w_ukv[l].astype(BF16),
                                             nb, seq, ctx_len, min(256, ctx_len))
        qm5 = qm.reshape(nb, MLA_HEADS, 1, seq + ctx_len, MLA_DQ)
        out_a = flash_attention(qm5, km, vm, seq, ctx_len, rows_q, tk_lat).reshape(n_lat, MIX_W)
        out_b = flash_attention(qg, kg, vg, seq, ctx_len, rows_q // qg.shape[2], tk_lat).reshape(n_lat, MIX_W)

        s5_tabs = s5_tables(ssm_lam_re[l], ssm_lam_im[l], ssm_log_dt[l], ssm_b_re[l], ssm_b_im[l], ssm_c_re[l], ssm_c_im[l], ssm_d[l])
        out_c = s5_mixer(pc, s5_tabs, ssm_w_glu[l].astype(BF16), nb, seq, ctx_len)

        out_d = fourier_mixer_fft(pd[:n_lat].reshape(nb, seq, MIX_W), fft_lat).reshape(n_lat, MIX_W)

        if need_ctx:
            ctx_a = context_attention(qm5, km, vm, seq, ctx_len).reshape(n_ctx, MIX_W)
            ctx_b = context_attention(qg, kg, vg, seq, ctx_len).reshape(n_ctx, MIX_W)
            ctx_d = fourier_mixer(pd[n_lat:].reshape(nb, ctx_len, MIX_W), dft_ctx, ctx_len, ctx_len).reshape(n_ctx, MIX_W)
            parts = [jnp.concatenate([out_a, ctx_a], axis=0), jnp.concatenate([out_b, ctx_b], axis=0), out_c,
                     jnp.concatenate([out_d, ctx_d], axis=0)]
            n_rows = n_lat + n_ctx
        else:
            parts = [out_a, out_b, out_c, out_d]
            n_rows = n_lat

        tok, h2, eid, gate = out_projection(parts, tok, w_out[l].astype(BF16), g1, norm2_g[l], sh2, sc2, rw, rb,
                                            n_rows, 2 * lat_tiles, nb, tm // 2)
        y = routed_moe(h2, eid, gate, wg_bf[l], wu_bf[l], wd_bf[l])
        tok = gated_residual(tok, y, g2, lat_tiles, nb, tm, None if need_ctx else final_norm_g)

    return tok[:n_lat].reshape(nb, seq, d)
```
